```python
import jax, jax.numpy as jnp
from jax import lax
import numpy as np

D_MODEL = 4096
BATCH = 2
SEQ = 8192
DEPTH = 1
DEC_BATCH = 8
DEC_SEQ = 16
PAST_LEN = 2048

CHUNK = 64
GLA_HEADS = 4
GLA_DK = D_MODEL // (2 * GLA_HEADS)
GLA_DV = D_MODEL // GLA_HEADS
GLA_KEY = GLA_HEADS * GLA_DK
GLA_VAL = GLA_HEADS * GLA_DV
GLA_GATE_RANK = 16
GLA_GATE_NORM = 16.0
SWA_HEADS = 64
SWA_KV_HEADS = 8
SWA_HEAD_DIM = D_MODEL // SWA_HEADS
SWA_GROUP = SWA_HEADS // SWA_KV_HEADS
WINDOW = 128
WINDOW_CHUNKS = WINDOW // CHUNK
ROPE_THETA = 10000.0
D_FF = -(-8 * D_MODEL // (3 * 256)) * 256
NORM_EPS = 1e-6
IN_SPLITS = (GLA_KEY, GLA_KEY, GLA_VAL, GLA_VAL, GLA_GATE_RANK,
             SWA_HEADS * SWA_HEAD_DIM, SWA_KV_HEADS * SWA_HEAD_DIM, SWA_KV_HEADS * SWA_HEAD_DIM,
             D_MODEL, D_MODEL)
N_IN = sum(IN_SPLITS)

kernel_name = 'gated_gla_swa_sink_streaming_step'


def rmsnorm(x, g):
    xf = x.astype(jnp.float32)
    y = xf * lax.rsqrt(jnp.mean(xf * xf, axis=-1, keepdims=True) + NORM_EPS)
    return (y * g.astype(jnp.float32)).astype(x.dtype)


def rope(x, pos):
    half = x.shape[-1] // 2
    inv = ROPE_THETA ** (-jnp.arange(half, dtype=jnp.float32) / half)
    ang = pos.astype(jnp.float32)[:, None] * inv[None, :]
    cos = jnp.cos(ang)[:, None, :]
    sin = jnp.sin(ang)[:, None, :]
    x1 = x[..., :half].astype(jnp.float32)
    x2 = x[..., half:].astype(jnp.float32)
    return jnp.concatenate([x1 * cos - x2 * sin, x2 * cos + x1 * sin], axis=-1).astype(x.dtype)


def gla_blocks(q, k, v, log_a, s0):
    C = q.shape[2]
    b = jnp.cumsum(log_a, axis=2)
    b_mid = b[:, :, C // 2:C // 2 + 1]
    b_last = b[:, :, -1]
    scores = jnp.einsum('bnihd,bnjhd->bnhij', q * jnp.exp(b - b_mid), k * jnp.exp(b_mid - b))
    scores = jnp.where(jnp.tril(jnp.ones((C, C), dtype=bool)), scores, 0.0)
    o_intra = jnp.einsum('bnhij,bnjhe->bnihe', scores, v)
    q_dec = q * jnp.exp(b)
    k_dec = k * jnp.exp(b_last[:, :, None] - b)

    def step(s, xs):
        qc, kc, vc, blc = xs
        o = jnp.einsum('bihd,bhde->bihe', qc, s)
        s = jnp.exp(blc)[..., None] * s + jnp.einsum('bjhd,bjhe->bhde', kc, vc)
        return s, o

    xs = (jnp.moveaxis(q_dec, 1, 0), jnp.moveaxis(k_dec, 1, 0), jnp.moveaxis(v, 1, 0), jnp.moveaxis(b_last, 1, 0))
    s_final, o_inter = lax.scan(step, s0.astype(jnp.float32), xs)
    return o_intra + jnp.moveaxis(o_inter, 0, 1), s_final


def sink_attention(q, k, v, sinks, mask):
    logits = jnp.einsum('bnqhgd,bnshd->bnhgqs', q, k).astype(jnp.float32) * (SWA_HEAD_DIM ** -0.5)
    if mask is not None:
        logits = jnp.where(mask, logits, -jnp.inf)
    sink = sinks.astype(jnp.float32).reshape(SWA_KV_HEADS, SWA_GROUP)[:, :, None, None]
    m = jnp.maximum(jnp.max(logits, axis=-1, keepdims=True), sink)
    p = jnp.exp(logits - m)
    p = p / (jnp.sum(p, axis=-1, keepdims=True) + jnp.exp(sink - m))
    return jnp.einsum('bnhgqs,bnshd->bnqhgd', p.astype(v.dtype), v)


def swa_banded(q, k, v, sinks):
    B, S = q.shape[:2]
    N = S // CHUNK
    qc = q.reshape(B, N, CHUNK, SWA_KV_HEADS, SWA_GROUP, SWA_HEAD_DIM)
    pad = ((0, 0), (WINDOW_CHUNKS, 0), (0, 0), (0, 0), (0, 0))
    kp = jnp.pad(k.reshape(B, N, CHUNK, SWA_KV_HEADS, SWA_HEAD_DIM), pad)
    vp = jnp.pad(v.reshape(B, N, CHUNK, SWA_KV_HEADS, SWA_HEAD_DIM), pad)
    kb = jnp.concatenate([kp[:, j:j + N] for j in range(WINDOW_CHUNKS + 1)], axis=2)
    vb = jnp.concatenate([vp[:, j:j + N] for j in range(WINDOW_CHUNKS + 1)], axis=2)
    key_pos = jnp.arange(N)[:, None] * CHUNK + jnp.arange((WINDOW_CHUNKS + 1) * CHUNK)[None, :] - WINDOW
    mask = (key_pos >= 0)[None, :, None, None, None, :]
    return sink_attention(qc, kb, vb, sinks, mask).reshape(B, S, D_MODEL)


def swa_cached(q, k_new, v_new, k_past, v_past, sinks):
    B, S = q.shape[:2]
    qc = q.reshape(B, 1, S, SWA_KV_HEADS, SWA_GROUP, SWA_HEAD_DIM)
    kb = jnp.concatenate([k_past, k_new.astype(k_past.dtype)], axis=1)[:, None]
    vb = jnp.concatenate([v_past, v_new.astype(v_past.dtype)], axis=1)[:, None]
    return sink_attention(qc, kb, vb, sinks, None).reshape(B, S, D_MODEL)


def trunk_layer(x, c, pos, s0, k_past, v_past, w_ada, b_ada, g_norm1, w_in, w_gk_up, b_gk,
                g_gla_out, swa_sinks, w_out, g_norm2, w_gate_up, w_down):
    dt = x.dtype
    B, S = x.shape[:2]
    mod = jnp.einsum('bd,de->be', jax.nn.silu(c), w_ada) + b_ada
    sh1, sc1, gt1, sh2, sc2, gt2 = [m[:, None, :] for m in jnp.split(mod, 6, axis=-1)]
    h = rmsnorm(x, g_norm1) * (1.0 + sc1) + sh1
    proj = jnp.einsum('bsd,de->bse', h, w_in)
    split_pts = [int(i) for i in np.cumsum(IN_SPLITS)[:-1]]
    qa, ka, va, ga, ra, qb, kb, vb, za, zb = jnp.split(proj, split_pts, axis=-1)

    log_a = jax.nn.log_sigmoid((jnp.einsum('bsr,rk->bsk', ra, w_gk_up) + b_gk).astype(jnp.float32)) / GLA_GATE_NORM
    blk_len = min(CHUNK, S)
    n_blk = S // blk_len

    def blk(t, dh):
        return t.reshape(B, n_blk, blk_len, GLA_HEADS, dh).astype(jnp.float32)

    o_a, s_new = gla_blocks(blk(qa * (GLA_DK ** -0.5), GLA_DK), blk(ka, GLA_DK), blk(va, GLA_DV),
                            blk(log_a, GLA_DK), s0)
    o_a = rmsnorm(o_a.reshape(B, S, GLA_HEADS, GLA_DV), g_gla_out) * jax.nn.silu(ga).reshape(B, S, GLA_HEADS, GLA_DV)
    o_a = o_a.reshape(B, S, D_MODEL)

    qb = rope(qb.reshape(B, S, SWA_HEADS, SWA_HEAD_DIM), pos)
    kb = rope(kb.reshape(B, S, SWA_KV_HEADS, SWA_HEAD_DIM), pos)
    vb = vb.reshape(B, S, SWA_KV_HEADS, SWA_HEAD_DIM)
    if k_past is None:
        o_b = swa_banded(qb, kb, vb, swa_sinks)
        k_keep, v_keep = kb[:, -WINDOW:], vb[:, -WINDOW:]
    else:
        o_b = swa_cached(qb, kb, vb, k_past, v_past, swa_sinks)
        k_keep, v_keep = kb, vb

    merged = jax.nn.sigmoid(za) * o_a + jax.nn.sigmoid(zb) * o_b
    x = x + gt1 * jnp.einsum('bsd,de->bse', merged, w_out)

    h2 = rmsnorm(x, g_norm2) * (1.0 + sc2) + sh2
    gate, up = jnp.split(jnp.einsum('bsd,df->bsf', h2, w_gate_up), 2, axis=-1)
    x = x + gt2 * jnp.einsum('bsf,fd->bsd', jax.nn.silu(gate) * up, w_down)
    return x.astype(dt), s_new, k_keep, v_keep


def setup_inputs(seed: int = 0) -> dict:
    key = jax.random.key(seed)
    ks = jax.random.split(key, 24)
    f32 = jnp.float32

    def nrm(k, shape, scale):
        return jax.random.normal(k, shape, f32) * scale

    D = D_MODEL
    return {
        'x_prompt': nrm(ks[0], (BATCH, SEQ, D), 1.0),
        'x_sample': nrm(ks[1], (DEC_BATCH, DEC_SEQ, D), 1.0),
        'state_gla': nrm(ks[2], (DEPTH, DEC_BATCH, GLA_HEADS, GLA_DK, GLA_DV), 1.0),
        'cache_swa_k': nrm(ks[3], (DEPTH, DEC_BATCH, WINDOW, SWA_KV_HEADS, SWA_HEAD_DIM), 1.0),
        'cache_swa_v': nrm(ks[4], (DEPTH, DEC_BATCH, WINDOW, SWA_KV_HEADS, SWA_HEAD_DIM), 1.0),
        'c_prompt': nrm(ks[5], (BATCH, D), 1.0),
        'c_sample': nrm(ks[6], (DEC_BATCH, D), 1.0),
        'w_ada': nrm(ks[7], (DEPTH, D, 6 * D), 0.5 * D ** -0.5),
        'b_ada': nrm(ks[8], (DEPTH, 6 * D), 0.01),
        'g_norm1': 1.0 + nrm(ks[9], (DEPTH, D), 0.01),
        'w_in': nrm(ks[10], (DEPTH, D, N_IN), D ** -0.5),
        'w_gk_up': nrm(ks[11], (DEPTH, GLA_GATE_RANK, GLA_KEY), GLA_GATE_RANK ** -0.5),
        'b_gk': nrm(ks[12], (DEPTH, GLA_KEY), 0.01),
        'g_gla_out': 1.0 + nrm(ks[13], (DEPTH, GLA_DV), 0.01),
        'swa_sinks': nrm(ks[14], (DEPTH, SWA_HEADS), 1.0),
        'w_out': nrm(ks[15], (DEPTH, D, D), D ** -0.5),
        'g_norm2': 1.0 + nrm(ks[16], (DEPTH, D), 0.01),
        'w_gate_up': nrm(ks[17], (DEPTH, D, 2 * D_FF), D ** -0.5),
        'w_down': nrm(ks[18], (DEPTH, D_FF, D), D_FF ** -0.5),
        'g_final': 1.0 + nrm(ks[19], (D,), 0.01),
    }


def reference(x_prompt, x_sample, state_gla, cache_swa_k, cache_swa_v, c_prompt, c_sample,
              w_ada, b_ada, g_norm1, w_in, w_gk_up, b_gk, g_gla_out, swa_sinks, w_out,
              g_norm2, w_gate_up, w_down, g_final):
    pos_p = jnp.arange(x_prompt.shape[1])
    pos_s = PAST_LEN + jnp.arange(x_sample.shape[1])
    s0_p = jnp.zeros((x_prompt.shape[0], GLA_HEADS, GLA_DK, GLA_DV), jnp.float32)
    hp, hs = x_prompt, x_sample
    sg_p, kc_p, vc_p, sg_s, kc_s, vc_s = [], [], [], [], [], []
    for l in range(DEPTH):
        hp, s_p, k_p, v_p = trunk_layer(hp, c_prompt, pos_p, s0_p, None, None,
                                        w_ada[l], b_ada[l], g_norm1[l], w_in[l], w_gk_up[l], b_gk[l],
                                        g_gla_out[l], swa_sinks[l], w_out[l], g_norm2[l],
                                        w_gate_up[l], w_down[l])
        hs, s_s, k_s, v_s = trunk_layer(hs, c_sample, pos_s, state_gla[l], cache_swa_k[l], cache_swa_v[l],
                                        w_ada[l], b_ada[l], g_norm1[l], w_in[l], w_gk_up[l], b_gk[l],
                                        g_gla_out[l], swa_sinks[l], w_out[l], g_norm2[l],
                                        w_gate_up[l], w_down[l])
        sg_p.append(s_p); kc_p.append(k_p); vc_p.append(v_p)
        sg_s.append(s_s); kc_s.append(k_s); vc_s.append(v_s)
    y_prompt = rmsnorm(hp, g_final)
    y_sample = rmsnorm(hs, g_final)
    return (y_prompt, y_sample, jnp.stack(sg_p), jnp.stack(kc_p), jnp.stack(vc_p),
            jnp.stack(sg_s), jnp.stack(kc_s), jnp.stack(vc_s))
```

```python
import functools

import jax
import jax.numpy as jnp
import numpy as np
from jax import lax
from jax.experimental import pallas as pl
from jax.experimental.pallas import tpu as pltpu

F32 = jnp.float32
BF16 = jnp.bfloat16

CHUNK = 64
GLA_HEADS = 4
GLA_GATE_RANK = 16
GLA_GATE_NORM = 16.0
SWA_HEADS = 64
SWA_KV_HEADS = 8
SWA_GROUP = SWA_HEADS // SWA_KV_HEADS
WINDOW = 128
ROPE_THETA = 10000.0
NORM_EPS = 1e-6
PAST_LEN = 2048

LANES = 128
VMEM_LIMIT_BYTES = 56 * 1024 * 1024


def _cparams(sem):
    return pltpu.CompilerParams(dimension_semantics=sem, vmem_limit_bytes=VMEM_LIMIT_BYTES)


def _silu(x):
    return x * (1.0 / (1.0 + jnp.exp(-x)))


def _sigmoid(x):
    return 1.0 / (1.0 + jnp.exp(-x))


def _pick(n, prefs):
    for p in prefs:
        if n % p == 0:
            return p
    return n


def _mod_kernel(c_ref, w_ref, b_ref, o_ref):
    a = _silu(c_ref[...]).astype(BF16)
    w = w_ref[...].astype(BF16)
    o_ref[...] = jnp.dot(a, w, preferred_element_type=F32) + b_ref[...]


def _mod_call(c_all, w_ada, b_ada):
    m, d = c_all.shape
    n = w_ada.shape[1]
    tn = _pick(n, (512, 256, 128))
    return pl.pallas_call(
        _mod_kernel,
        grid=(n // tn,),
        in_specs=[pl.BlockSpec((m, d), lambda j: (0, 0)),
                  pl.BlockSpec((d, tn), lambda j: (0, j)),
                  pl.BlockSpec((1, tn), lambda j: (0, j))],
        out_specs=pl.BlockSpec((m, tn), lambda j: (0, j)),
        out_shape=jax.ShapeDtypeStruct((m, n), F32),
        compiler_params=_cparams(("arbitrary",)),
        name="adaln_mod",
    )(c_all, w_ada, b_ada.reshape(1, n))


def _norm_kernel(x_ref, g_ref, *rest, modulate):
    o_ref = rest[-1]
    x = x_ref[...]
    y = x * lax.rsqrt(jnp.mean(x * x, axis=-1, keepdims=True) + NORM_EPS) * g_ref[...]
    if modulate:
        sc_ref, sh_ref = rest[0], rest[1]
        y = y * (1.0 + sc_ref[...]) + sh_ref[...]
    o_ref[...] = y.astype(o_ref.dtype)


def _row_vec_spec(rows_per_group, tr, d):
    tiles = rows_per_group // tr
    return pl.BlockSpec((None, 1, d), lambda i: (i // tiles, 0, 0))


def _norm_call(x, g, sc, sh, rows_per_group, out_dtype):
    t, d = x.shape
    tr = _pick(rows_per_group, (256, 128, 64, 32, 16, 8))
    modulate = sc is not None
    in_specs = [pl.BlockSpec((tr, d), lambda i: (i, 0)), pl.BlockSpec((1, d), lambda i: (0, 0))]
    args = [x, g.reshape(1, d)]
    if modulate:
        in_specs += [_row_vec_spec(rows_per_group, tr, d)] * 2
        args += [sc, sh]
    return pl.pallas_call(
        functools.partial(_norm_kernel, modulate=modulate),
        grid=(t // tr,),
        in_specs=in_specs,
        out_specs=pl.BlockSpec((tr, d), lambda i: (i, 0)),
        out_shape=jax.ShapeDtypeStruct((t, d), out_dtype),
        compiler_params=_cparams(("parallel",)),
        name="rmsnorm_mod",
    )(*args)


def _mm_kernel(a_ref, w_ref, *rest, nk, epilogue):
    if nk > 1:
        acc_ref = rest[-1]
        rest = rest[:-1]
    o_ref = rest[-1]
    extras = rest[:-1]

    def finish(acc):
        if epilogue == "residual":
            res_ref, gate_ref = extras
            out = res_ref[...] + gate_ref[...] * acc
        elif epilogue == "swiglu":
            half = acc.shape[1] // 2
            out = _silu(acc[:, :half]) * acc[:, half:]
        else:
            out = acc
        o_ref[...] = out.astype(o_ref.dtype)

    part = jnp.dot(a_ref[...], w_ref[...], preferred_element_type=F32)
    if nk == 1:
        finish(part)
    else:
        k = pl.program_id(2)

        @pl.when(k == 0)
        def _():
            acc_ref[...] = part

        @pl.when(jnp.logical_and(k > 0, k < nk - 1))
        def _():
            acc_ref[...] += part

        @pl.when(k == nk - 1)
        def _():
            finish(acc_ref[...] + part)


def _mm_call(a, w, *, tm, tn, tk=None, epilogue="plain", res=None, gate=None, rows_per_group=None,
             out_dtype=F32, name="matmul"):
    m, kdim = a.shape
    n = w.shape[1]
    tk = kdim if tk is None else tk
    nk = kdim // tk
    n_out = n // 2 if epilogue == "swiglu" else n
    tn_out = tn // 2 if epilogue == "swiglu" else tn
    in_specs = [pl.BlockSpec((tm, tk), lambda i, j, k: (i, k)),
                pl.BlockSpec((tk, tn), lambda i, j, k: (k, j))]
    args = [a, w]
    if epilogue == "residual":
        in_specs.append(pl.BlockSpec((tm, tn), lambda i, j, k: (i, j)))
        if rows_per_group % tm == 0:
            tiles = rows_per_group // tm
            in_specs.append(pl.BlockSpec((None, 1, tn), lambda i, j, k: (i // tiles, 0, j)))
        else:
            gate = jnp.repeat(gate[:, 0, :], rows_per_group, axis=0)
            in_specs.append(pl.BlockSpec((tm, tn), lambda i, j, k: (i, j)))
        args += [res, gate]
    scratch = [pltpu.VMEM((tm, tn), F32)] if nk > 1 else []
    return pl.pallas_call(
        functools.partial(_mm_kernel, nk=nk, epilogue=epilogue),
        grid=(m // tm, n // tn, nk),
        in_specs=in_specs,
        out_specs=pl.BlockSpec((tm, tn_out), lambda i, j, k: (i, j)),
        out_shape=jax.ShapeDtypeStruct((m, n_out), out_dtype),
        scratch_shapes=scratch,
        compiler_params=_cparams(("parallel", "parallel", "arbitrary")),
        name=name,
    )(*args)


def _rope_tables(pos, head_dim):
    half = head_dim // 2
    inv = ROPE_THETA ** (-jnp.arange(half, dtype=F32) / half)
    ang = pos.astype(F32)[:, None] * inv[None, :]
    cos, sin = jnp.cos(ang), jnp.sin(ang)
    reps = LANES // head_dim
    cos_t = jnp.tile(jnp.concatenate([cos, cos], axis=1), (1, reps))
    sin_t = jnp.tile(jnp.concatenate([-sin, sin], axis=1), (1, reps))
    return cos_t, sin_t


def _rope_block(x, cos_t, sin_t, first_half):
    partner = jnp.where(first_half, pltpu.roll(x, 96, 1), pltpu.roll(x, 32, 1))
    return x * cos_t + partner * sin_t


def _first_half_mask(rows):
    lane = lax.broadcasted_iota(jnp.int32, (rows, LANES), 1)
    return (lane % 64) < 32


def _rope_k_kernel(k_ref, cos_ref, sin_ref, o_ref):
    rows, width = k_ref.shape
    fh = _first_half_mask(rows)
    cos_t, sin_t = cos_ref[...], sin_ref[...]
    for c in range(width // LANES):
        sl = slice(c * LANES, (c + 1) * LANES)
        o_ref[:, sl] = _rope_block(k_ref[:, sl], cos_t, sin_t, fh)


def _rope_k_call(proj, col_block, width, cos_t, sin_t, rows_per_group):
    t = proj.shape[0]
    tr = _pick(rows_per_group, (256, 128, 64, 32, 16, 8))
    tiles = rows_per_group // tr
    return pl.pallas_call(
        _rope_k_kernel,
        grid=(t // tr,),
        in_specs=[pl.BlockSpec((tr, width), lambda i: (i, col_block)),
                  pl.BlockSpec((tr, LANES), lambda i: (i % tiles, 0)),
                  pl.BlockSpec((tr, LANES), lambda i: (i % tiles, 0))],
        out_specs=pl.BlockSpec((tr, width), lambda i: (i, 0)),
        out_shape=jax.ShapeDtypeStruct((t, width), F32),
        compiler_params=_cparams(("parallel",)),
        name="rope_k",
    )(proj, cos_t, sin_t)


def _split_dot(a, b_f32):
    hi = b_f32.astype(BF16)
    lo = (b_f32 - hi.astype(F32)).astype(BF16)
    return (jnp.dot(a, hi, preferred_element_type=F32) + jnp.dot(a, lo, preferred_element_type=F32))


def _gla_kernel(q_ref, k_ref, v_ref, ga_ref, ra_ref, wgk_ref, bgk_ref, g_ref, s0_ref,
                o_ref, sout_ref, s_scr, *, n_chunks, scale):
    n = pl.program_id(2)
    c, dk = q_ref.shape
    dv = v_ref.shape[1]

    @pl.when(n == 0)
    def _():
        s_scr[...] = s0_ref[...]

    ra = ra_ref[...]
    w = wgk_ref[...]
    ra_hi = ra.astype(BF16)
    ra_lo = (ra - ra_hi.astype(F32)).astype(BF16)
    z = _split_dot(ra_hi, w) + jnp.dot(ra_lo, w.astype(BF16), preferred_element_type=F32) + bgk_ref[...]
    log_a = (jnp.minimum(z, 0.0) - jnp.log(1.0 + jnp.exp(-jnp.abs(z)))) * (1.0 / GLA_GATE_NORM)

    row = lax.broadcasted_iota(jnp.int32, (c, c), 0)
    col = lax.broadcasted_iota(jnp.int32, (c, c), 1)
    causal = row >= col
    tril = jnp.where(causal, 1.0, 0.0).astype(BF16)
    b = _split_dot(tril, log_a)
    b_mid = b[c // 2:c // 2 + 1, :]
    b_last = b[c - 1:c, :]

    q = q_ref[...] * scale
    k = k_ref[...]
    v = v_ref[...].astype(BF16)
    qs = (q * jnp.exp(b - b_mid)).astype(BF16)
    ks = (k * jnp.exp(b_mid - b)).astype(BF16)
    scores = lax.dot_general(qs, ks, (((1,), (1,)), ((), ())), preferred_element_type=F32)
    scores = jnp.where(causal, scores, 0.0).astype(BF16)
    o = jnp.dot(scores, v, preferred_element_type=F32)

    s_old = s_scr[...]
    qd = (q * jnp.exp(b)).astype(BF16)
    o = o + jnp.dot(qd, s_old.astype(BF16), preferred_element_type=F32)

    kd = (k * jnp.exp(b_last - b)).astype(BF16)
    upd = lax.dot_general(kd, v, (((0,), (0,)), ((), ())), preferred_element_type=F32)
    dec_col = jnp.transpose(jnp.broadcast_to(jnp.exp(b_last), (LANES, dk)))
    for j in range(dv // LANES):
        sl = slice(j * LANES, (j + 1) * LANES)
        s_scr[:, sl] = s_old[:, sl] * dec_col + upd[:, sl]

    on = o * lax.rsqrt(jnp.mean(o * o, axis=-1, keepdims=True) + NORM_EPS) * g_ref[...]
    o_ref[...] = (on * _silu(ga_ref[...])).astype(o_ref.dtype)

    @pl.when(n == n_chunks - 1)
    def _():
        sout_ref[...] = s_scr[...]


def _gla_call(proj1, ra, w_gk_up, b_gk, g_gla_out, s0, batch, seq):
    h = GLA_HEADS
    dk = w_gk_up.shape[1] // h
    dv = g_gla_out.shape[0]
    c = min(CHUNK, seq)
    nc = seq // c
    t = batch * seq
    qb, kb = dk, dk
    v_off = (2 * h * dk) // dv
    g_off = v_off + h
    row = lambda b, hh, n: b * nc + n
    kern = functools.partial(_gla_kernel, n_chunks=nc, scale=dk ** -0.5)
    return pl.pallas_call(
        kern,
        grid=(batch, h, nc),
        in_specs=[
            pl.BlockSpec((c, dk), lambda b, hh, n: (row(b, hh, n), hh)),
            pl.BlockSpec((c, dk), lambda b, hh, n: (row(b, hh, n), h + hh)),
            pl.BlockSpec((c, dv), lambda b, hh, n: (row(b, hh, n), v_off + hh)),
            pl.BlockSpec((c, dv), lambda b, hh, n: (row(b, hh, n), g_off + hh)),
            pl.BlockSpec((c, LANES), lambda b, hh, n: (row(b, hh, n), 0)),
            pl.BlockSpec((LANES, dk), lambda b, hh, n: (0, hh)),
            pl.BlockSpec((1, dk), lambda b, hh, n: (0, hh)),
            pl.BlockSpec((1, dv), lambda b, hh, n: (0, 0)),
            pl.BlockSpec((None, None, dk, dv), lambda b, hh, n: (b, hh, 0, 0)),
        ],
        out_specs=[
            pl.BlockSpec((c, dv), lambda b, hh, n: (row(b, hh, n), hh)),
            pl.BlockSpec((None, None, dk, dv), lambda b, hh, n: (b, hh, 0, 0)),
        ],
        out_shape=[jax.ShapeDtypeStruct((t, h * dv), F32),
                   jax.ShapeDtypeStruct((batch, h, dk, dv), F32)],
        scratch_shapes=[pltpu.VMEM((dk, dv), F32)],
        compiler_params=_cparams(("parallel", "parallel", "arbitrary")),
        name="gla_chunked",
    )(proj1, proj1, proj1, proj1, ra, w_gk_up, b_gk.reshape(1, -1), g_gla_out.reshape(1, -1), s0)


def _dup_half(x, half, lane_lo):
    rolled = pltpu.roll(x, 64, 1)
    return jnp.where(lane_lo == (half == 0), x, rolled)


def _swa_kernel(*refs, tq, n_ka, masked):
    sinks_ref, q_ref, cos_ref, sin_ref = refs[0:4]
    ka_refs = refs[4:4 + n_ka]
    va_refs = refs[4 + n_ka:4 + 2 * n_ka]
    kb_ref, vb_ref, oa_ref, za_ref, zb_ref, o_ref = refs[4 + 2 * n_ka:]

    hd = LANES // 2
    fh = _first_half_mask(tq)
    cos_t, sin_t = cos_ref[...], sin_ref[...]
    q_scale = hd ** -0.5

    lane_q = lax.broadcasted_iota(jnp.int32, (tq, LANES), 1) < hd
    n_a = sum(r.shape[0] for r in ka_refs)
    lane_a = lax.broadcasted_iota(jnp.int32, (n_a, LANES), 1) < hd
    lane_b = lane_q
    row_g = lax.broadcasted_iota(jnp.int32, (SWA_GROUP * tq, 1), 0) // tq

    if masked:
        n = pl.program_id(1)
        key_idx = lax.broadcasted_iota(jnp.int32, (SWA_GROUP * tq, n_a), 1)
        valid_a = key_idx >= (n_a - n * tq)

    nt = (((1,), (1,)), ((), ()))
    for h in range(SWA_KV_HEADS):
        pc, half = h // 2, h % 2
        sl = slice(pc * LANES, (pc + 1) * LANES)
        ka = jnp.concatenate([r[:, sl] for r in ka_refs], axis=0) if n_ka > 1 else ka_refs[0][:, sl]
        va = jnp.concatenate([r[:, sl] for r in va_refs], axis=0) if n_ka > 1 else va_refs[0][:, sl]
        k2a = _dup_half(ka, half, lane_a).astype(BF16)
        v2a = _dup_half(va, half, lane_a).astype(BF16)
        k2b = _dup_half(kb_ref[:, sl], half, lane_b).astype(BF16)
        v2b = _dup_half(vb_ref[:, sl], half, lane_b).astype(BF16)

        rows = []
        sink = jnp.full((SWA_GROUP * tq, 1), sinks_ref[h * SWA_GROUP], F32)
        for g in range(SWA_GROUP):
            cb = h * (SWA_GROUP // 2) + g // 2
            qsl = slice(cb * LANES, (cb + 1) * LANES)
            qr = _rope_block(q_ref[:, qsl], cos_t, sin_t, fh) * q_scale
            rows.append(jnp.where(lane_q == (g % 2 == 0), qr, 0.0))
            if g > 0:
                sink = jnp.where(row_g == g, sinks_ref[h * SWA_GROUP + g], sink)
        lhs = jnp.concatenate(rows, axis=0).astype(BF16)

        la = lax.dot_general(lhs, k2a, nt, preferred_element_type=F32)
        lb = lax.dot_general(lhs, k2b, nt, preferred_element_type=F32)
        if masked:
            la = jnp.where(valid_a, la, -jnp.inf)
        m = jnp.maximum(jnp.maximum(jnp.max(la, axis=-1, keepdims=True),
                                    jnp.max(lb, axis=-1, keepdims=True)), sink)
        pa = jnp.exp(la - m)
        pb = jnp.exp(lb - m)
        den = (jnp.sum(pa, axis=-1, keepdims=True) + jnp.sum(pb, axis=-1, keepdims=True)
               + jnp.exp(sink - m))
        out2 = (jnp.dot(pa.astype(BF16), v2a, preferred_element_type=F32)
                + jnp.dot(pb.astype(BF16), v2b, preferred_element_type=F32)) * (1.0 / den)

        for j in range(SWA_GROUP // 2):
            cb = h * (SWA_GROUP // 2) + j
            osl = slice(cb * LANES, (cb + 1) * LANES)
            ob = jnp.where(lane_q, out2[(2 * j) * tq:(2 * j + 1) * tq, :],
                           out2[(2 * j + 1) * tq:(2 * j + 2) * tq, :])
            merged = _sigmoid(za_ref[:, osl]) * oa_ref[:, osl] + _sigmoid(zb_ref[:, osl]) * ob
            o_ref[:, osl] = merged.astype(o_ref.dtype)


def _swa_call(proj2, k_rot, o_a, sinks, cos_t, sin_t, batch, seq, cache_k=None, cache_v=None):
    t, d = o_a.shape
    kvw = k_rot.shape[1]
    v_blk = (3 * d) // kvw + 1
    if cache_k is None:
        tq = CHUNK
        nq = seq // tq
        rowi = lambda b, n: b * nq + n
        prev = lambda b, n, back: b * nq + jnp.maximum(n - back, 0)
        n_ka = WINDOW // CHUNK
        ka_args = [k_rot] * n_ka
        va_args = [proj2] * n_ka
        ka_specs = [pl.BlockSpec((tq, kvw), functools.partial(lambda b, n, back: (prev(b, n, back), 0),
                                                               back=n_ka - i)) for i in range(n_ka)]
        va_specs = [pl.BlockSpec((tq, kvw), functools.partial(lambda b, n, back: (prev(b, n, back), v_blk),
                                                               back=n_ka - i)) for i in range(n_ka)]
        masked = True
    else:
        tq = seq
        nq = 1
        rowi = lambda b, n: b
        n_ka = 1
        ka_args, va_args = [cache_k], [cache_v]
        ka_specs = [pl.BlockSpec((None, WINDOW, kvw), lambda b, n: (b, 0, 0))]
        va_specs = [pl.BlockSpec((None, WINDOW, kvw), lambda b, n: (b, 0, 0))]
        masked = False
    wide = lambda blk: pl.BlockSpec((tq, d), lambda b, n: (rowi(b, n), blk))
    tab = pl.BlockSpec((tq, LANES), lambda b, n: (n, 0))
    in_specs = ([pl.BlockSpec(memory_space=pltpu.SMEM), wide(0), tab, tab] + ka_specs + va_specs +
                [pl.BlockSpec((tq, kvw), lambda b, n: (rowi(b, n), 0)),
                 pl.BlockSpec((tq, kvw), lambda b, n: (rowi(b, n), v_blk)),
                 wide(0), wide(1), wide(2)])
    args = [sinks, proj2, cos_t, sin_t] + ka_args + va_args + [k_rot, proj2, o_a, proj2, proj2]
    return pl.pallas_call(
        functools.partial(_swa_kernel, tq=tq, n_ka=n_ka, masked=masked),
        grid=(batch, nq),
        in_specs=in_specs,
        out_specs=pl.BlockSpec((tq, d), lambda b, n: (rowi(b, n), 0)),
        out_shape=jax.ShapeDtypeStruct((t, d), BF16),
        compiler_params=_cparams(("parallel", "arbitrary")),
        name="swa_merge",
    )(*args)


def _prep_weights(w_in, w_out, w_gate_up, w_down, d):
    h = GLA_HEADS
    key = (d // (2 * h)) * h
    n1 = 2 * key + 2 * d
    r0 = n1 + GLA_GATE_RANK
    kvw = SWA_KV_HEADS * (d // SWA_HEADS)
    q0, k0, v0, za0, zb0 = r0, r0 + d, r0 + d + kvw, r0 + d + 2 * kvw, r0 + 2 * d + 2 * kvw
    w1 = w_in[:, :n1].astype(BF16)
    w_r = jnp.pad(w_in[:, n1:r0], ((0, 0), (0, LANES - GLA_GATE_RANK))).astype(BF16)
    w2 = jnp.concatenate([w_in[:, q0:k0], w_in[:, za0:zb0], w_in[:, zb0:zb0 + d],
                          w_in[:, k0:v0], w_in[:, v0:za0]], axis=1).astype(BF16)
    dff = w_down.shape[0]
    tn_ff = 512
    dff_p = -(-dff // tn_ff) * tn_ff
    gate = jnp.pad(w_gate_up[:, :dff], ((0, 0), (0, dff_p - dff))).reshape(d, dff_p // tn_ff, 1, tn_ff)
    up = jnp.pad(w_gate_up[:, dff:], ((0, 0), (0, dff_p - dff))).reshape(d, dff_p // tn_ff, 1, tn_ff)
    w_gu = jnp.concatenate([gate, up], axis=2).reshape(d, 2 * dff_p).astype(BF16)
    w_dn = jnp.pad(w_down, ((0, dff_p - dff), (0, 0))).astype(BF16)
    return w1, w_r, w2, w_out.astype(BF16), w_gu, w_dn, tn_ff


def _layer(x, mod, pos, s0, cache_k, cache_v, weights, params, batch, seq):
    w1, w_r, w2, w_o, w_gu, w_dn, tn_ff = weights
    g_norm1, w_gk_up, b_gk, g_gla_out, sinks, g_norm2 = params
    t, d = x.shape
    sh1, sc1, gt1, sh2, sc2, gt2 = [m.reshape(batch, 1, d) for m in jnp.split(mod, 6, axis=-1)]
    kvw = SWA_KV_HEADS * (d // SWA_HEADS)
    tm = min(1024, t)

    h = _norm_call(x, g_norm1, sc1, sh1, seq, BF16)
    proj1 = _mm_call(h, w1, tm=tm, tn=1024, name="in_proj_gla")
    ra = _mm_call(h, w_r, tm=tm, tn=LANES, name="in_proj_rank")
    proj2 = _mm_call(h, w2, tm=tm, tn=1024, name="in_proj_swa")

    wgk_pad = jnp.pad(w_gk_up, ((0, LANES - GLA_GATE_RANK), (0, 0)))
    o_a, s_new = _gla_call(proj1, ra, wgk_pad, b_gk, g_gla_out, s0, batch, seq)

    cos_t, sin_t = _rope_tables(pos, d // SWA_HEADS)
    k_rot = _rope_k_call(proj2, (3 * d) // kvw, kvw, cos_t, sin_t, seq)
    merged = _swa_call(proj2, k_rot, o_a, sinks, cos_t, sin_t, batch, seq, cache_k, cache_v)

    x1 = _mm_call(merged, w_o, tm=tm, tn=512, epilogue="residual", res=x, gate=gt1,
                  rows_per_group=seq, name="out_proj")
    h2 = _norm_call(x1, g_norm2, sc2, sh2, seq, BF16)
    act = _mm_call(h2, w_gu, tm=tm, tn=2 * tn_ff, epilogue="swiglu", out_dtype=BF16,
                   name="ffn_gate_up")
    kff = w_dn.shape[0]
    x2 = _mm_call(act, w_dn, tm=tm, tn=1024, tk=kff // 4, epilogue="residual", res=x1, gate=gt2,
                  rows_per_group=seq, name="ffn_down")

    v_new = proj2[:, 3 * d + kvw:]
    hd = d // SWA_HEADS
    if cache_k is None:
        k_keep = k_rot.reshape(batch, seq, SWA_KV_HEADS, hd)[:, -WINDOW:]
        v_keep = v_new.reshape(batch, seq, SWA_KV_HEADS, hd)[:, -WINDOW:]
    else:
        k_keep = k_rot.reshape(batch, seq, SWA_KV_HEADS, hd)
        v_keep = v_new.reshape(batch, seq, SWA_KV_HEADS, hd)
    return x2, s_new, k_keep, v_keep


def kernel(x_prompt, x_sample, state_gla, cache_swa_k, cache_swa_v, c_prompt, c_sample, w_ada, b_ada,
           g_norm1, w_in, w_gk_up, b_gk, g_gla_out, swa_sinks, w_out, g_norm2, w_gate_up, w_down, g_final):
    bp, sp, d = x_prompt.shape
    bs, ss, _ = x_sample.shape
    depth = w_ada.shape[0]
    dk = w_gk_up.shape[2] // GLA_HEADS
    dv = g_gla_out.shape[1]
    kvw = SWA_KV_HEADS * (d // SWA_HEADS)

    pos_p = jnp.arange(sp)
    pos_s = PAST_LEN + jnp.arange(ss)
    s0_p = jnp.zeros((bp, GLA_HEADS, dk, dv), F32)

    nc = bp + bs
    nc_pad = -(-nc // 16) * 16
    c_all = jnp.concatenate([c_prompt, c_sample, jnp.zeros((nc_pad - nc, d), F32)], axis=0)

    hp = x_prompt.reshape(bp * sp, d)
    hs = x_sample.reshape(bs * ss, d)
    outs = [[] for _ in range(6)]
    for l in range(depth):
        mod = _mod_call(c_all, w_ada[l], b_ada[l])
        weights = _prep_weights(w_in[l], w_out[l], w_gate_up[l], w_down[l], d)
        params = (g_norm1[l], w_gk_up[l], b_gk[l], g_gla_out[l], swa_sinks[l], g_norm2[l])
        hp, s_p, k_p, v_p = _layer(hp, mod[:bp], pos_p, s0_p, None, None, weights, params, bp, sp)
        hs, s_s, k_s, v_s = _layer(hs, mod[bp:nc], pos_s, state_gla[l],
                                   cache_swa_k[l].reshape(bs, WINDOW, kvw),
                                   cache_swa_v[l].reshape(bs, WINDOW, kvw), weights, params, bs, ss)
        for lst, val in zip(outs, (s_p, k_p, v_p, s_s, k_s, v_s)):
            lst.append(val)
    y_prompt = _norm_call(hp, g_final, None, None, sp, F32).reshape(bp, sp, d)
    y_sample = _norm_call(hs, g_final, None, None, ss, F32).reshape(bs, ss, d)
    return (y_prompt, y_sample) + tuple(jnp.stack(o) for o in outs)
```

```python
import functools

import jax
import jax.numpy as jnp
import numpy as np
from jax import lax
from jax.experimental import pallas as pl
from jax.experimental.pallas import tpu as pltpu

F32 = jnp.float32
BF16 = jnp.bfloat16

CHUNK = 64
GLA_HEADS = 4
GLA_GATE_RANK = 16
GLA_GATE_NORM = 16.0
SWA_HEADS = 64
SWA_KV_HEADS = 8
SWA_GROUP = SWA_HEADS // SWA_KV_HEADS
WINDOW = 128
ROPE_THETA = 10000.0
NORM_EPS = 1e-6
PAST_LEN = 2048

LANES = 128
VMEM_LIMIT_BYTES = 56 * 1024 * 1024
GLA_BLOCK_CHUNKS = 2
FF_TILE = 512


def _cparams(sem):
    return pltpu.CompilerParams(dimension_semantics=sem, vmem_limit_bytes=VMEM_LIMIT_BYTES)


def _silu(x):
    return x * (1.0 / (1.0 + jnp.exp(-x)))


def _sigmoid(x):
    return 1.0 / (1.0 + jnp.exp(-x))


def _pick(n, prefs):
    for p in prefs:
        if n % p == 0:
            return p
    return n


def _relayout_kernel(src_ref, nxt_ref, mode_ref, main, nxt, o_ref, *, shift):
    b = pl.program_id(1)
    mode = mode_ref[b]
    w = o_ref.shape[1]

    @pl.when(mode == 0)
    def _():
        o_ref[...] = main[...].astype(BF16)

    if shift:
        @pl.when(mode == 1)
        def _():
            x = jnp.concatenate([main[...], nxt[...]], axis=1)
            o_ref[...] = x[:, shift:shift + w].astype(BF16)

    @pl.when(mode == 2)
    def _():
        o_ref[...] = jnp.zeros(o_ref.shape, BF16)


def _relayout_call(src, width, src_blk, mode, shift=0, tr=None):
    r = src.shape[0]
    tr = r if tr is None else tr
    nb = len(src_blk)
    sub = width // LANES
    src_blk = np.asarray(src_blk, np.int32)
    nxt_blk = (src_blk + 1) * sub if shift else np.zeros(nb, np.int32)
    grid_spec = pltpu.PrefetchScalarGridSpec(
        num_scalar_prefetch=3,
        grid=(r // tr, nb),
        in_specs=[pl.BlockSpec((tr, width), lambda i, b, s, nx, m: (i, s[b])),
                  pl.BlockSpec((tr, LANES), lambda i, b, s, nx, m: (i, nx[b]))],
        out_specs=pl.BlockSpec((tr, width), lambda i, b, s, nx, m: (i, b)),
    )
    return pl.pallas_call(
        functools.partial(_relayout_kernel, shift=shift),
        grid_spec=grid_spec,
        out_shape=jax.ShapeDtypeStruct((r, nb * width), BF16),
        compiler_params=_cparams(("parallel", "arbitrary")),
        name="weight_relayout",
    )(jnp.asarray(src_blk), jnp.asarray(nxt_blk, jnp.int32), jnp.asarray(np.asarray(mode, np.int32)), src, src)


def _mod_kernel(c_ref, w_ref, b_ref, o_ref):
    a = _silu(c_ref[...]).astype(BF16)
    w = w_ref[...].astype(BF16)
    o_ref[...] = jnp.dot(a, w, preferred_element_type=F32) + b_ref[...]


def _mod_call(c_all, w_ada, b_ada):
    m, d = c_all.shape
    n = w_ada.shape[1]
    tn = _pick(n, (512, 256, 128))
    return pl.pallas_call(
        _mod_kernel,
        grid=(n // tn,),
        in_specs=[pl.BlockSpec((m, d), lambda j: (0, 0)),
                  pl.BlockSpec((d, tn), lambda j: (0, j)),
                  pl.BlockSpec((1, tn), lambda j: (0, j))],
        out_specs=pl.BlockSpec((m, tn), lambda j: (0, j)),
        out_shape=jax.ShapeDtypeStruct((m, n), F32),
        compiler_params=_cparams(("arbitrary",)),
        name="adaln_mod",
    )(c_all, w_ada, b_ada.reshape(1, n))


def _norm_kernel(x_ref, g_ref, *rest, modulate):
    o_ref = rest[-1]
    x = x_ref[...]
    y = x * lax.rsqrt(jnp.mean(x * x, axis=-1, keepdims=True) + NORM_EPS) * g_ref[...]
    if modulate:
        sc_ref, sh_ref = rest[0], rest[1]
        y = y * (1.0 + sc_ref[...]) + sh_ref[...]
    o_ref[...] = y.astype(o_ref.dtype)


def _norm_call(x, g, sc, sh, rows_per_group, out_dtype):
    t, d = x.shape
    tr = _pick(rows_per_group, (256, 128, 64, 32, 16, 8))
    tiles = rows_per_group // tr
    modulate = sc is not None
    in_specs = [pl.BlockSpec((tr, d), lambda i: (i, 0)), pl.BlockSpec((1, d), lambda i: (0, 0))]
    args = [x, g.reshape(1, d)]
    if modulate:
        in_specs += [pl.BlockSpec((None, 1, d), lambda i: (i // tiles, 0, 0))] * 2
        args += [sc, sh]
    return pl.pallas_call(
        functools.partial(_norm_kernel, modulate=modulate),
        grid=(t // tr,),
        in_specs=in_specs,
        out_specs=pl.BlockSpec((tr, d), lambda i: (i, 0)),
        out_shape=jax.ShapeDtypeStruct((t, d), out_dtype),
        compiler_params=_cparams(("parallel",)),
        name="rmsnorm_mod",
    )(*args)


def _mm_kernel(a_ref, w_ref, *rest, nk, epilogue):
    if nk > 1:
        acc_ref = rest[-1]
        rest = rest[:-1]
    o_ref = rest[-1]
    extras = rest[:-1]

    def finish(acc):
        if epilogue == "residual":
            res_ref, gate_ref = extras
            out = res_ref[...] + gate_ref[...] * acc
        elif epilogue == "swiglu":
            half = acc.shape[1] // 2
            out = _silu(acc[:, :half]) * acc[:, half:]
        else:
            out = acc
        o_ref[...] = out.astype(o_ref.dtype)

    part = jnp.dot(a_ref[...], w_ref[...], preferred_element_type=F32)
    if nk == 1:
        finish(part)
    else:
        k = pl.program_id(2)

        @pl.when(k == 0)
        def _():
            acc_ref[...] = part

        @pl.when(jnp.logical_and(k > 0, k < nk - 1))
        def _():
            acc_ref[...] += part

        @pl.when(k == nk - 1)
        def _():
            finish(acc_ref[...] + part)


def _mm_call(a, w, *, tm, tn, tk=None, epilogue="plain", res=None, gate=None, rows_per_group=None,
             out_dtype=F32, name="matmul"):
    m = a.shape[0]
    kdim, n = w.shape
    tk = kdim if tk is None else tk
    nk = kdim // tk
    n_out = n // 2 if epilogue == "swiglu" else n
    tn_out = tn // 2 if epilogue == "swiglu" else tn
    in_specs = [pl.BlockSpec((tm, tk), lambda i, j, k: (i, k)),
                pl.BlockSpec((tk, tn), lambda i, j, k: (k, j))]
    args = [a, w]
    if epilogue == "residual":
        in_specs.append(pl.BlockSpec((tm, tn), lambda i, j, k: (i, j)))
        if rows_per_group % tm == 0:
            tiles = rows_per_group // tm
            in_specs.append(pl.BlockSpec((None, 1, tn), lambda i, j, k: (i // tiles, 0, j)))
        else:
            gate = jnp.repeat(gate[:, 0, :], rows_per_group, axis=0)
            in_specs.append(pl.BlockSpec((tm, tn), lambda i, j, k: (i, j)))
        args += [res, gate]
    scratch = [pltpu.VMEM((tm, tn), F32)] if nk > 1 else []
    return pl.pallas_call(
        functools.partial(_mm_kernel, nk=nk, epilogue=epilogue),
        grid=(m // tm, n // tn, nk),
        in_specs=in_specs,
        out_specs=pl.BlockSpec((tm, tn_out), lambda i, j, k: (i, j)),
        out_shape=jax.ShapeDtypeStruct((m, n_out), out_dtype),
        scratch_shapes=scratch,
        compiler_params=_cparams(("parallel", "parallel", "arbitrary")),
        name=name,
    )(*args)


def _rope_tables(pos, head_dim):
    half = head_dim // 2
    inv = ROPE_THETA ** (-jnp.arange(half, dtype=F32) / half)
    ang = pos.astype(F32)[:, None] * inv[None, :]
    cos, sin = jnp.cos(ang), jnp.sin(ang)
    reps = LANES // head_dim
    cos_t = jnp.tile(jnp.concatenate([cos, cos], axis=1), (1, reps))
    sin_t = jnp.tile(jnp.concatenate([-sin, sin], axis=1), (1, reps))
    return cos_t, sin_t


def _rope_block(x, cos_t, sin_t, first_half):
    partner = jnp.where(first_half, pltpu.roll(x, 96, 1), pltpu.roll(x, 32, 1))
    return x * cos_t + partner * sin_t


def _first_half_mask(rows):
    lane = lax.broadcasted_iota(jnp.int32, (rows, LANES), 1)
    return (lane % 64) < 32


def _rope_k_kernel(k_ref, cos_ref, sin_ref, o_ref):
    rows, width = k_ref.shape
    fh = _first_half_mask(rows)
    cos_t, sin_t = cos_ref[...], sin_ref[...]
    for c in range(width // LANES):
        sl = slice(c * LANES, (c + 1) * LANES)
        o_ref[:, sl] = _rope_block(k_ref[:, sl].astype(F32), cos_t, sin_t, fh)


def _rope_k_call(proj, col_block, width, cos_t, sin_t, rows_per_group):
    t = proj.shape[0]
    tr = _pick(rows_per_group, (256, 128, 64, 32, 16, 8))
    tiles = rows_per_group // tr
    return pl.pallas_call(
        _rope_k_kernel,
        grid=(t // tr,),
        in_specs=[pl.BlockSpec((tr, width), lambda i: (i, col_block)),
                  pl.BlockSpec((tr, LANES), lambda i: (i % tiles, 0)),
                  pl.BlockSpec((tr, LANES), lambda i: (i % tiles, 0))],
        out_specs=pl.BlockSpec((tr, width), lambda i: (i, 0)),
        out_shape=jax.ShapeDtypeStruct((t, width), F32),
        compiler_params=_cparams(("parallel",)),
        name="rope_k",
    )(proj, cos_t, sin_t)


def _gla_kernel(q_ref, k_ref, v_ref, ga_ref, ra_ref, wgk_ref, bgk_ref, g_ref, s0_ref,
                o_ref, sout_ref, s_scr, *, n_steps, c, scale):
    n = pl.program_id(2)
    rows, dk = q_ref.shape
    dv = v_ref.shape[1]
    nr = rows // c
    nt = (((1,), (1,)), ((), ()))

    @pl.when(n == 0)
    def _():
        s_scr[...] = s0_ref[...]

    ra = ra_ref[...]
    lane = lax.broadcasted_iota(jnp.int32, ra.shape, 1)
    ra_hi = ra.astype(BF16)
    ra_lo = (ra - ra_hi.astype(F32)).astype(BF16)
    lhs = jnp.where(jnp.logical_and(lane >= GLA_GATE_RANK, lane < 2 * GLA_GATE_RANK), ra_lo, ra_hi)
    z = jnp.dot(lhs, wgk_ref[...], preferred_element_type=F32) + bgk_ref[...]
    log_a = (jnp.minimum(z, 0.0) - jnp.log(1.0 + jnp.exp(-jnp.abs(z)))) * (1.0 / GLA_GATE_NORM)

    ri = lax.broadcasted_iota(jnp.int32, (rows, 2 * rows), 0)
    ci = lax.broadcasted_iota(jnp.int32, (rows, 2 * rows), 1) % rows
    tril2 = jnp.where(jnp.logical_and(ri // c == ci // c, ci <= ri), 1.0, 0.0).astype(BF16)
    la_hi = log_a.astype(BF16)
    la_lo = (log_a - la_hi.astype(F32)).astype(BF16)
    b_all = jnp.dot(tril2, jnp.concatenate([la_hi, la_lo], axis=0), preferred_element_type=F32)

    row = lax.broadcasted_iota(jnp.int32, (c, c), 0)
    col = lax.broadcasted_iota(jnp.int32, (c, c), 1)
    causal = row >= col

    s_old = s_scr[...]
    s_bf = s_old.astype(BF16)
    vs, qds, kds, kd32, o_parts = [], [], [], [], []
    cum = [None]
    for r in range(nr):
        sl = slice(r * c, (r + 1) * c)
        b = b_all[sl, :]
        b_mid = b[c // 2:c // 2 + 1, :]
        b_last = b[c - 1:c, :]
        q = q_ref[sl, :].astype(F32) * scale
        k = k_ref[sl, :].astype(F32)
        v = v_ref[sl, :]
        qs = (q * jnp.exp(b - b_mid)).astype(BF16)
        ks = (k * jnp.exp(b_mid - b)).astype(BF16)
        sc = lax.dot_general(qs, ks, nt, preferred_element_type=F32)
        o = jnp.dot(jnp.where(causal, sc, 0.0).astype(BF16), v, preferred_element_type=F32)
        qd = q * jnp.exp(b)
        for rp in range(r):
            fac = cum[r] - cum[rp + 1] if rp + 1 < r else None
            qx = qd if fac is None else qd * jnp.exp(fac)
            sx = lax.dot_general(qx.astype(BF16), kds[rp], nt, preferred_element_type=F32)
            o = o + jnp.dot(sx.astype(BF16), vs[rp], preferred_element_type=F32)
        qds.append(qd if cum[r] is None else qd * jnp.exp(cum[r]))
        kd = k * jnp.exp(b_last - b)
        kd32.append(kd)
        kds.append(kd.astype(BF16))
        vs.append(v)
        o_parts.append(o)
        cum.append(b_last if cum[r] is None else cum[r] + b_last)

    q_all = jnp.concatenate(qds, axis=0).astype(BF16) if nr > 1 else qds[0].astype(BF16)
    o_inter = jnp.dot(q_all, s_bf, preferred_element_type=F32)
    total = cum[nr]
    k_parts = []
    for r in range(nr):
        if r + 1 < nr:
            k_parts.append((kd32[r] * jnp.exp(total - cum[r + 1])).astype(BF16))
        else:
            k_parts.append(kds[r])
    k_all = jnp.concatenate(k_parts, axis=0) if nr > 1 else k_parts[0]
    v_all = v_ref[...]
    upd = lax.dot_general(k_all, v_all, (((0,), (0,)), ((), ())), preferred_element_type=F32)
    dec_col = jnp.transpose(jnp.broadcast_to(jnp.exp(total), (LANES, dk)))
    for j in range(dv // LANES):
        sl = slice(j * LANES, (j + 1) * LANES)
        s_scr[:, sl] = s_old[:, sl] * dec_col + upd[:, sl]

    g = g_ref[...]
    for r in range(nr):
        sl = slice(r * c, (r + 1) * c)
        o = o_parts[r] + o_inter[sl, :]
        on = o * lax.rsqrt(jnp.mean(o * o, axis=-1, keepdims=True) + NORM_EPS) * g
        o_ref[sl, :] = (on * _silu(ga_ref[sl, :].astype(F32))).astype(o_ref.dtype)

    @pl.when(n == n_steps - 1)
    def _():
        sout_ref[...] = s_scr[...]


def _gla_call(proj, ra3, wgk3, b_gk, g_gla_out, s0, batch, seq):
    h = GLA_HEADS
    dk = wgk3.shape[1] // h
    dv = g_gla_out.shape[0]
    c = min(CHUNK, seq)
    rows = c * GLA_BLOCK_CHUNKS if seq % (c * GLA_BLOCK_CHUNKS) == 0 else c
    ns = seq // rows
    t = batch * seq
    v_off = (2 * h * dk) // dv
    g_off = v_off + h
    row = lambda b, hh, n: b * ns + n
    kern = functools.partial(_gla_kernel, n_steps=ns, c=c, scale=dk ** -0.5)
    return pl.pallas_call(
        kern,
        grid=(batch, h, ns),
        in_specs=[
            pl.BlockSpec((rows, dk), lambda b, hh, n: (row(b, hh, n), hh)),
            pl.BlockSpec((rows, dk), lambda b, hh, n: (row(b, hh, n), h + hh)),
            pl.BlockSpec((rows, dv), lambda b, hh, n: (row(b, hh, n), v_off + hh)),
            pl.BlockSpec((rows, dv), lambda b, hh, n: (row(b, hh, n), g_off + hh)),
            pl.BlockSpec((rows, LANES), lambda b, hh, n: (row(b, hh, n), 0)),
            pl.BlockSpec((LANES, dk), lambda b, hh, n: (0, hh)),
            pl.BlockSpec((1, dk), lambda b, hh, n: (0, hh)),
            pl.BlockSpec((1, dv), lambda b, hh, n: (0, 0)),
            pl.BlockSpec((None, None, dk, dv), lambda b, hh, n: (b, hh, 0, 0)),
        ],
        out_specs=[
            pl.BlockSpec((rows, dv), lambda b, hh, n: (row(b, hh, n), hh)),
            pl.BlockSpec((None, None, dk, dv), lambda b, hh, n: (b, hh, 0, 0)),
        ],
        out_shape=[jax.ShapeDtypeStruct((t, h * dv), BF16),
                   jax.ShapeDtypeStruct((batch, h, dk, dv), F32)],
        scratch_shapes=[pltpu.VMEM((dk, dv), F32)],
        compiler_params=_cparams(("parallel", "parallel", "arbitrary")),
        name="gla_chunked",
    )(proj, proj, proj, proj, ra3, wgk3, b_gk.reshape(1, -1), g_gla_out.reshape(1, -1), s0)


def _dup_half(x, half, lane_lo):
    rolled = pltpu.roll(x, 64, 1)
    return jnp.where(lane_lo == (half == 0), x, rolled)


def _swa_kernel(*refs, tq, n_ka, masked):
    sinks_ref, q_ref, cos_ref, sin_ref = refs[0:4]
    ka_refs = refs[4:4 + n_ka]
    va_refs = refs[4 + n_ka:4 + 2 * n_ka]
    kb_ref, vb_ref, oa_ref, za_ref, zb_ref, o_ref = refs[4 + 2 * n_ka:]

    hd = LANES // 2
    grows = SWA_GROUP * tq
    fh = _first_half_mask(tq)
    cos_t, sin_t = cos_ref[...], sin_ref[...]
    q_scale = hd ** -0.5

    lane_q = lax.broadcasted_iota(jnp.int32, (tq, LANES), 1) < hd
    lane_a = lax.broadcasted_iota(jnp.int32, (WINDOW, LANES), 1) < hd
    lane2 = lax.broadcasted_iota(jnp.int32, (grows, LANES), 1)
    ones = jnp.ones((2 * LANES, LANES), BF16)
    zpad = jnp.zeros((LANES - tq, LANES), BF16)

    if masked:
        n = pl.program_id(1)
        valid_a = lane2 >= (WINDOW - n * tq)

    nt = (((1,), (1,)), ((), ()))
    for h in range(SWA_KV_HEADS):
        pc, half = h // 2, h % 2
        sl = slice(pc * LANES, (pc + 1) * LANES)
        ka = jnp.concatenate([r[:, sl] for r in ka_refs], axis=0) if n_ka > 1 else ka_refs[0][:, sl]
        va = jnp.concatenate([r[:, sl] for r in va_refs], axis=0) if n_ka > 1 else va_refs[0][:, sl]
        k_all = jnp.concatenate([_dup_half(ka.astype(F32), half, lane_a).astype(BF16),
                                 _dup_half(kb_ref[:, sl].astype(F32), half, lane_q).astype(BF16), zpad], axis=0)
        v_all = jnp.concatenate([_dup_half(va.astype(F32), half, lane_a).astype(BF16),
                                 _dup_half(vb_ref[:, sl].astype(F32), half, lane_q).astype(BF16), zpad], axis=0)

        rows, sink_rows = [], []
        for g in range(SWA_GROUP):
            cb = h * (SWA_GROUP // 2) + g // 2
            qsl = slice(cb * LANES, (cb + 1) * LANES)
            qr = _rope_block(q_ref[:, qsl].astype(F32), cos_t, sin_t, fh) * q_scale
            rows.append(jnp.where(lane_q == (g % 2 == 0), qr, 0.0).astype(BF16))
            sink_rows.append(jnp.full((tq, LANES), sinks_ref[h * SWA_GROUP + g], F32))
        lhs = jnp.concatenate(rows, axis=0)
        sink_b = jnp.concatenate(sink_rows, axis=0)

        logits = lax.dot_general(lhs, k_all, nt, preferred_element_type=F32)
        l1 = logits[:, :LANES]
        l2 = logits[:, LANES:]
        if masked:
            l1 = jnp.where(valid_a, l1, -jnp.inf)
        l2 = jnp.where(lane2 < tq, l2, jnp.where(lane2 == tq, sink_b, -jnp.inf))
        m = jnp.max(jnp.maximum(l1, l2), axis=-1, keepdims=True)
        p = jnp.concatenate([jnp.exp(l1 - m), jnp.exp(l2 - m)], axis=1).astype(BF16)
        pv = jnp.dot(p, v_all, preferred_element_type=F32)
        den = jnp.dot(p, ones, preferred_element_type=F32)
        out2 = pv / den

        for j in range(SWA_GROUP // 2):
            cb = h * (SWA_GROUP // 2) + j
            osl = slice(cb * LANES, (cb + 1) * LANES)
            ob = jnp.where(lane_q, out2[(2 * j) * tq:(2 * j + 1) * tq, :],
                           out2[(2 * j + 1) * tq:(2 * j + 2) * tq, :])
            merged = (_sigmoid(za_ref[:, osl].astype(F32)) * oa_ref[:, osl].astype(F32)
                      + _sigmoid(zb_ref[:, osl].astype(F32)) * ob)
            o_ref[:, osl] = merged.astype(o_ref.dtype)


def _swa_call(proj, q_blk, k_rot, o_a, sinks, cos_t, sin_t, batch, seq, cache_k=None, cache_v=None):
    t, d = o_a.shape
    kvw = k_rot.shape[1]
    v_blk = ((q_blk + 3) * d) // kvw + 1
    if cache_k is None:
        tq = CHUNK
        nq = seq // tq
        rowi = lambda b, n: b * nq + n
        prev = lambda b, n, back: b * nq + jnp.maximum(n - back, 0)
        n_ka = WINDOW // CHUNK
        ka_args = [k_rot] * n_ka
        va_args = [proj] * n_ka
        ka_specs = [pl.BlockSpec((tq, kvw), functools.partial(lambda b, n, back: (prev(b, n, back), 0),
                                                               back=n_ka - i)) for i in range(n_ka)]
        va_specs = [pl.BlockSpec((tq, kvw), functools.partial(lambda b, n, back: (prev(b, n, back), v_blk),
                                                               back=n_ka - i)) for i in range(n_ka)]
        masked = True
    else:
        tq = seq
        nq = 1
        rowi = lambda b, n: b
        n_ka = 1
        ka_args, va_args = [cache_k], [cache_v]
        ka_specs = [pl.BlockSpec((None, WINDOW, kvw), lambda b, n: (b, 0, 0))]
        va_specs = [pl.BlockSpec((None, WINDOW, kvw), lambda b, n: (b, 0, 0))]
        masked = False
    wide = lambda blk: pl.BlockSpec((tq, d), lambda b, n: (rowi(b, n), blk))
    tab = pl.BlockSpec((tq, LANES), lambda b, n: (n, 0))
    in_specs = ([pl.BlockSpec(memory_space=pltpu.SMEM), wide(q_blk), tab, tab] + ka_specs + va_specs +
                [pl.BlockSpec((tq, kvw), lambda b, n: (rowi(b, n), 0)),
                 pl.BlockSpec((tq, kvw), lambda b, n: (rowi(b, n), v_blk)),
                 wide(0), wide(q_blk + 1), wide(q_blk + 2)])
    args = [sinks, proj, cos_t, sin_t] + ka_args + va_args + [k_rot, proj, o_a, proj, proj]
    return pl.pallas_call(
        functools.partial(_swa_kernel, tq=tq, n_ka=n_ka, masked=masked),
        grid=(batch, nq),
        in_specs=in_specs,
        out_specs=pl.BlockSpec((tq, d), lambda b, n: (rowi(b, n), 0)),
        out_shape=jax.ShapeDtypeStruct((t, d), BF16),
        compiler_params=_cparams(("parallel", "arbitrary")),
        name="swa_merge",
    )(*args)


def _prep_weights(w_in, w_out, w_gate_up, w_down, w_gk_up, d):
    h = GLA_HEADS
    key = (d // (2 * h)) * h
    n1 = 2 * key + 2 * d
    kvw = SWA_KV_HEADS * (d // SWA_HEADS)
    w = kvw
    nb1 = n1 // w
    per_d = d // w
    q_s = list(range(per_d))
    k_s, v_s = per_d, per_d + 1
    za_s = [per_d + 2 + i for i in range(per_d)]
    zb_s = [2 * per_d + 2 + i for i in range(per_d)]
    shifted = q_s + za_s + zb_s + [k_s, v_s]
    src_blk = list(range(nb1)) + [nb1 + s for s in shifted]
    mode = [0] * nb1 + [1] * len(shifted)
    w12 = _relayout_call(w_in, w, src_blk, mode, shift=GLA_GATE_RANK, tr=w_in.shape[0] // 2)

    ra_cols = w_in[:, n1:n1 + GLA_GATE_RANK]
    w_r = jnp.pad(jnp.concatenate([ra_cols] * 3, axis=1),
                  ((0, 0), (0, LANES - 3 * GLA_GATE_RANK))).astype(BF16)
    wg_hi = w_gk_up.astype(BF16)
    wg_lo = (w_gk_up - wg_hi.astype(F32)).astype(BF16)
    wgk3 = jnp.pad(jnp.concatenate([wg_hi, wg_hi, wg_lo], axis=0),
                   ((0, LANES - 3 * GLA_GATE_RANK), (0, 0)))

    w_o = _relayout_call(w_out, 256, list(range(w_out.shape[1] // 256)), [0] * (w_out.shape[1] // 256))

    dff = w_down.shape[0]
    sub = 256
    n_src = dff // sub
    n_tiles = -(-dff // FF_TILE)
    per_tile = FF_TILE // sub
    src_blk, mode = [], []
    for j in range(n_tiles):
        for part in range(2):
            for s in range(per_tile):
                blk = j * per_tile + s
                ok = blk < n_src
                src_blk.append(part * n_src + blk if ok else 0)
                mode.append(0 if ok else 2)
    w_gu = _relayout_call(w_gate_up, sub, src_blk, mode)
    w_dn = _relayout_call(w_down, 256, list(range(d // 256)), [0] * (d // 256), tr=dff // 2)
    return w12, w_r, wgk3, w_o, w_gu, w_dn


def _layer(x, mod, pos, s0, cache_k, cache_v, weights, params, batch, seq):
    w12, w_r, wgk3, w_o, w_gu, w_dn = weights
    g_norm1, b_gk, g_gla_out, sinks, g_norm2 = params
    t, d = x.shape
    sh1, sc1, gt1, sh2, sc2, gt2 = [m.reshape(batch, 1, d) for m in jnp.split(mod, 6, axis=-1)]
    kvw = SWA_KV_HEADS * (d // SWA_HEADS)
    hd = d // SWA_HEADS
    tm = min(1024, t)
    q_blk = (w12.shape[1] - 3 * d - 2 * kvw) // d

    h = _norm_call(x, g_norm1, sc1, sh1, seq, BF16)
    proj = _mm_call(h, w12, tm=tm, tn=1024, out_dtype=BF16, name="in_proj")
    ra3 = _mm_call(h, w_r, tm=tm, tn=LANES, name="in_proj_rank")

    o_a, s_new = _gla_call(proj, ra3, wgk3, b_gk, g_gla_out, s0, batch, seq)

    cos_t, sin_t = _rope_tables(pos, hd)
    k_blk = ((q_blk + 3) * d) // kvw
    k_rot = _rope_k_call(proj, k_blk, kvw, cos_t, sin_t, seq)
    merged = _swa_call(proj, q_blk, k_rot, o_a, sinks, cos_t, sin_t, batch, seq, cache_k, cache_v)

    x1 = _mm_call(merged, w_o, tm=tm, tn=512, epilogue="residual", res=x, gate=gt1,
                  rows_per_group=seq, name="out_proj")
    h2 = _norm_call(x1, g_norm2, sc2, sh2, seq, BF16)
    act = _mm_call(h2, w_gu, tm=tm, tn=2 * FF_TILE, epilogue="swiglu", out_dtype=BF16, name="ffn_gate_up")
    x2 = _mm_call(act, w_dn, tm=tm, tn=512, tk=w_dn.shape[0] // 2, epilogue="residual", res=x1, gate=gt2,
                  rows_per_group=seq, name="ffn_down")

    keep = WINDOW if cache_k is None else seq
    k_keep = k_rot.reshape(batch, seq, SWA_KV_HEADS, hd)[:, seq - keep:]
    v_new = proj.reshape(batch, seq, -1)[:, seq - keep:, (k_blk + 1) * kvw:(k_blk + 2) * kvw]
    v_keep = v_new.astype(F32).reshape(batch, keep, SWA_KV_HEADS, hd)
    return x2, s_new, k_keep, v_keep


def kernel(x_prompt, x_sample, state_gla, cache_swa_k, cache_swa_v, c_prompt, c_sample, w_ada, b_ada,
           g_norm1, w_in, w_gk_up, b_gk, g_gla_out, swa_sinks, w_out, g_norm2, w_gate_up, w_down, g_final):
    bp, sp, d = x_prompt.shape
    bs, ss, _ = x_sample.shape
    depth = w_ada.shape[0]
    dk = w_gk_up.shape[2] // GLA_HEADS
    dv = g_gla_out.shape[1]
    kvw = SWA_KV_HEADS * (d // SWA_HEADS)

    pos_p = jnp.arange(sp)
    pos_s = PAST_LEN + jnp.arange(ss)
    s0_p = jnp.zeros((bp, GLA_HEADS, dk, dv), F32)

    nc = bp + bs
    nc_pad = -(-nc // 16) * 16
    c_all = jnp.concatenate([c_prompt, c_sample, jnp.zeros((nc_pad - nc, d), F32)], axis=0)

    hp = x_prompt.reshape(bp * sp, d)
    hs = x_sample.reshape(bs * ss, d)
    outs = [[] for _ in range(6)]
    for l in range(depth):
        mod = _mod_call(c_all, w_ada[l], b_ada[l])
        weights = _prep_weights(w_in[l], w_out[l], w_gate_up[l], w_down[l], w_gk_up[l], d)
        params = (g_norm1[l], b_gk[l], g_gla_out[l], swa_sinks[l], g_norm2[l])
        hp, s_p, k_p, v_p = _layer(hp, mod[:bp], pos_p, s0_p, None, None, weights, params, bp, sp)
        hs, s_s, k_s, v_s = _layer(hs, mod[bp:nc], pos_s, state_gla[l],
                                   cache_swa_k[l].reshape(bs, WINDOW, kvw),
                                   cache_swa_v[l].reshape(bs, WINDOW, kvw), weights, params, bs, ss)
        for lst, val in zip(outs, (s_p, k_p, v_p, s_s, k_s, v_s)):
            lst.append(val)
    y_prompt = _norm_call(hp, g_final, None, None, sp, F32).reshape(bp, sp, d)
    y_sample = _norm_call(hs, g_final, None, None, ss, F32).reshape(bs, ss, d)
    return (y_prompt, y_sample) + tuple(jnp.stack(o) for o in outs)
```

```python
import functools

import jax
import jax.numpy as jnp
import numpy as np
from jax import lax
from jax.experimental import pallas as pl
from jax.experimental.pallas import tpu as pltpu

F32 = jnp.float32
BF16 = jnp.bfloat16

CHUNK = 64
GLA_HEADS = 4
GLA_GATE_RANK = 16
GLA_GATE_NORM = 16.0
SWA_HEADS = 64
SWA_KV_HEADS = 8
SWA_GROUP = SWA_HEADS // SWA_KV_HEADS
WINDOW = 128
ROPE_THETA = 10000.0
NORM_EPS = 1e-6
PAST_LEN = 2048
LOG2E = 1.4426950408889634

LANES = 128
VMEM_LIMIT_BYTES = 56 * 1024 * 1024
GLA_HEADS_PER_STEP = 4
GLA_BLOCK_CHUNKS = 2
SWA_HEADS_PER_STAGE = 4
FF_TILE = 512


def _cparams(sem):
    return pltpu.CompilerParams(dimension_semantics=sem, vmem_limit_bytes=VMEM_LIMIT_BYTES)


def _silu(x):
    return x * (1.0 / (1.0 + jnp.exp(-x)))


def _sigmoid(x):
    return 1.0 / (1.0 + jnp.exp(-x))


def _pick(n, prefs):
    for p in prefs:
        if n % p == 0:
            return p
    return n


def _relayout_kernel(src_ref, nxt_ref, mode_ref, main, nxt, o_ref, *, shift):
    b = pl.program_id(1)
    mode = mode_ref[b]
    w = o_ref.shape[1]

    @pl.when(mode == 0)
    def _():
        o_ref[...] = main[...].astype(BF16)

    if shift:
        @pl.when(mode == 1)
        def _():
            x = jnp.concatenate([main[...], nxt[...]], axis=1)
            o_ref[...] = x[:, shift:shift + w].astype(BF16)

    @pl.when(mode == 2)
    def _():
        o_ref[...] = jnp.zeros(o_ref.shape, BF16)


def _relayout_call(src, width, src_blk, mode, shift=0, tr=None):
    r = src.shape[0]
    tr = r if tr is None else tr
    nb = len(src_blk)
    sub = width // LANES
    src_blk = np.asarray(src_blk, np.int32)
    nxt_blk = (src_blk + 1) * sub if shift else np.zeros(nb, np.int32)
    grid_spec = pltpu.PrefetchScalarGridSpec(
        num_scalar_prefetch=3,
        grid=(r // tr, nb),
        in_specs=[pl.BlockSpec((tr, width), lambda i, b, s, nx, m: (i, s[b])),
                  pl.BlockSpec((tr, LANES), lambda i, b, s, nx, m: (i, nx[b]))],
        out_specs=pl.BlockSpec((tr, width), lambda i, b, s, nx, m: (i, b)),
    )
    return pl.pallas_call(
        functools.partial(_relayout_kernel, shift=shift),
        grid_spec=grid_spec,
        out_shape=jax.ShapeDtypeStruct((r, nb * width), BF16),
        compiler_params=_cparams(("parallel", "arbitrary")),
        name="weight_relayout",
    )(jnp.asarray(src_blk), jnp.asarray(nxt_blk, jnp.int32), jnp.asarray(np.asarray(mode, np.int32)), src, src)


def _mod_kernel(c_ref, w_ref, b_ref, o_ref):
    a = _silu(c_ref[...]).astype(BF16)
    w = w_ref[...].astype(BF16)
    o_ref[...] = jnp.dot(a, w, preferred_element_type=F32) + b_ref[...]


def _mod_call(c_all, w_ada, b_ada):
    m, d = c_all.shape
    n = w_ada.shape[1]
    tn = _pick(n, (512, 256, 128))
    return pl.pallas_call(
        _mod_kernel,
        grid=(n // tn,),
        in_specs=[pl.BlockSpec((m, d), lambda j: (0, 0)),
                  pl.BlockSpec((d, tn), lambda j: (0, j)),
                  pl.BlockSpec((1, tn), lambda j: (0, j))],
        out_specs=pl.BlockSpec((m, tn), lambda j: (0, j)),
        out_shape=jax.ShapeDtypeStruct((m, n), F32),
        compiler_params=_cparams(("arbitrary",)),
        name="adaln_mod",
    )(c_all, w_ada, b_ada.reshape(1, n))


def _norm_kernel(x_ref, g_ref, *rest, modulate):
    o_ref = rest[-1]
    x = x_ref[...]
    y = x * lax.rsqrt(jnp.mean(x * x, axis=-1, keepdims=True) + NORM_EPS) * g_ref[...]
    if modulate:
        sc_ref, sh_ref = rest[0], rest[1]
        y = y * (1.0 + sc_ref[...]) + sh_ref[...]
    o_ref[...] = y.astype(o_ref.dtype)


def _norm_call(x, g, sc, sh, rows_per_group, out_dtype):
    t, d = x.shape
    tr = _pick(rows_per_group, (256, 128, 64, 32, 16, 8))
    tiles = rows_per_group // tr
    modulate = sc is not None
    in_specs = [pl.BlockSpec((tr, d), lambda i: (i, 0)), pl.BlockSpec((1, d), lambda i: (0, 0))]
    args = [x, g.reshape(1, d)]
    if modulate:
        in_specs += [pl.BlockSpec((None, 1, d), lambda i: (i // tiles, 0, 0))] * 2
        args += [sc, sh]
    return pl.pallas_call(
        functools.partial(_norm_kernel, modulate=modulate),
        grid=(t // tr,),
        in_specs=in_specs,
        out_specs=pl.BlockSpec((tr, d), lambda i: (i, 0)),
        out_shape=jax.ShapeDtypeStruct((t, d), out_dtype),
        compiler_params=_cparams(("parallel",)),
        name="rmsnorm_mod",
    )(*args)


def _mm_kernel(a_ref, w_ref, *rest, nk, epilogue):
    if nk > 1:
        acc_ref = rest[-1]
        rest = rest[:-1]
    o_ref = rest[-1]
    extras = rest[:-1]

    def finish(acc):
        if epilogue == "residual":
            res_ref, gate_ref = extras
            out = res_ref[...] + gate_ref[...] * acc
        elif epilogue == "swiglu":
            half = acc.shape[1] // 2
            out = _silu(acc[:, :half]) * acc[:, half:]
        else:
            out = acc
        o_ref[...] = out.astype(o_ref.dtype)

    part = jnp.dot(a_ref[...], w_ref[...], preferred_element_type=F32)
    if nk == 1:
        finish(part)
    else:
        k = pl.program_id(2)

        @pl.when(k == 0)
        def _():
            acc_ref[...] = part

        @pl.when(jnp.logical_and(k > 0, k < nk - 1))
        def _():
            acc_ref[...] += part

        @pl.when(k == nk - 1)
        def _():
            finish(acc_ref[...] + part)


def _mm_call(a, w, *, tm, tn, tk=None, epilogue="plain", res=None, gate=None, rows_per_group=None,
             out_dtype=F32, name="matmul"):
    m = a.shape[0]
    kdim, n = w.shape
    tk = kdim if tk is None else tk
    nk = kdim // tk
    n_out = n // 2 if epilogue == "swiglu" else n
    tn_out = tn // 2 if epilogue == "swiglu" else tn
    in_specs = [pl.BlockSpec((tm, tk), lambda i, j, k: (i, k)),
                pl.BlockSpec((tk, tn), lambda i, j, k: (k, j))]
    args = [a, w]
    if epilogue == "residual":
        in_specs.append(pl.BlockSpec((tm, tn), lambda i, j, k: (i, j)))
        if rows_per_group % tm == 0:
            tiles = rows_per_group // tm
            in_specs.append(pl.BlockSpec((None, 1, tn), lambda i, j, k: (i // tiles, 0, j)))
        else:
            gate = jnp.repeat(gate[:, 0, :], rows_per_group, axis=0)
            in_specs.append(pl.BlockSpec((tm, tn), lambda i, j, k: (i, j)))
        args += [res, gate]
    scratch = [pltpu.VMEM((tm, tn), F32)] if nk > 1 else []
    return pl.pallas_call(
        functools.partial(_mm_kernel, nk=nk, epilogue=epilogue),
        grid=(m // tm, n // tn, nk),
        in_specs=in_specs,
        out_specs=pl.BlockSpec((tm, tn_out), lambda i, j, k: (i, j)),
        out_shape=jax.ShapeDtypeStruct((m, n_out), out_dtype),
        scratch_shapes=scratch,
        compiler_params=_cparams(("parallel", "parallel", "arbitrary")),
        name=name,
    )(*args)


def _rope_tables(pos, head_dim):
    half = head_dim // 2
    inv = ROPE_THETA ** (-jnp.arange(half, dtype=F32) / half)
    ang = pos.astype(F32)[:, None] * inv[None, :]
    cos, sin = jnp.cos(ang), jnp.sin(ang)
    reps = LANES // head_dim
    cos_t = jnp.tile(jnp.concatenate([cos, cos], axis=1), (1, reps))
    sin_t = jnp.tile(jnp.concatenate([-sin, sin], axis=1), (1, reps))
    return cos_t, sin_t


def _rope_block(x, cos_t, sin_t, first_half):
    partner = jnp.where(first_half, pltpu.roll(x, 96, 1), pltpu.roll(x, 32, 1))
    return x * cos_t + partner * sin_t


def _first_half_mask(rows):
    lane = lax.broadcasted_iota(jnp.int32, (rows, LANES), 1)
    return (lane % 64) < 32


def _rope_k_kernel(k_ref, cos_ref, sin_ref, o_ref):
    rows, width = k_ref.shape
    fh = _first_half_mask(rows)
    cos_t, sin_t = cos_ref[...], sin_ref[...]
    for c in range(width // LANES):
        sl = slice(c * LANES, (c + 1) * LANES)
        o_ref[:, sl] = _rope_block(k_ref[:, sl].astype(F32), cos_t, sin_t, fh)


def _rope_k_call(proj, col_block, width, cos_t, sin_t, rows_per_group):
    t = proj.shape[0]
    tr = _pick(rows_per_group, (256, 128, 64, 32, 16, 8))
    tiles = rows_per_group // tr
    return pl.pallas_call(
        _rope_k_kernel,
        grid=(t // tr,),
        in_specs=[pl.BlockSpec((tr, width), lambda i: (i, col_block)),
                  pl.BlockSpec((tr, LANES), lambda i: (i % tiles, 0)),
                  pl.BlockSpec((tr, LANES), lambda i: (i % tiles, 0))],
        out_specs=pl.BlockSpec((tr, width), lambda i: (i, 0)),
        out_shape=jax.ShapeDtypeStruct((t, width), F32),
        compiler_params=_cparams(("parallel",)),
        name="rope_k",
    )(proj, cos_t, sin_t)


def _gla_kernel(q_ref, k_ref, v_ref, ga_ref, ra_ref, wgk_ref, bgk_ref, g_ref, s0_ref,
                o_ref, sout_ref, s_scr, *, n_steps, c, scale, hp):
    n = pl.program_id(2)
    rows = q_ref.shape[0]
    dk = q_ref.shape[1] // hp
    dv = v_ref.shape[1] // hp
    nr = rows // c
    nt = (((1,), (1,)), ((), ()))
    tn = (((0,), (0,)), ((), ()))
    heads = range(hp)
    ksl = lambda u: slice(u * dk, (u + 1) * dk)
    vsl = lambda u: slice(u * dv, (u + 1) * dv)

    @pl.when(n == 0)
    def _():
        s_scr[...] = s0_ref[...]

    ra = ra_ref[...]
    lane = lax.broadcasted_iota(jnp.int32, ra.shape, 1)
    ra_hi = ra.astype(BF16)
    ra_lo = (ra - ra_hi.astype(F32)).astype(BF16)
    lhs = jnp.where(jnp.logical_and(lane >= GLA_GATE_RANK, lane < 2 * GLA_GATE_RANK), ra_lo, ra_hi)
    z = jnp.dot(lhs, wgk_ref[...], preferred_element_type=F32) + bgk_ref[...]
    log_a = (jnp.minimum(z, 0.0) - jnp.log(1.0 + jnp.exp(-jnp.abs(z)))) * (1.0 / GLA_GATE_NORM)

    ri = lax.broadcasted_iota(jnp.int32, (rows, 2 * rows), 0)
    ci = lax.broadcasted_iota(jnp.int32, (rows, 2 * rows), 1) % rows
    tril2 = jnp.where(jnp.logical_and(ri // c == ci // c, ci <= ri), 1.0, 0.0).astype(BF16)
    la_hi = log_a.astype(BF16)
    la_lo = (log_a - la_hi.astype(F32)).astype(BF16)
    b_all = jnp.dot(tril2, jnp.concatenate([la_hi, la_lo], axis=0), preferred_element_type=F32)

    row = lax.broadcasted_iota(jnp.int32, (c, c), 0)
    col = lax.broadcasted_iota(jnp.int32, (c, c), 1)
    causal = row >= col

    vs, qds, kds, kd32, o_parts = [], [], [], [], []
    cum = [None]
    for r in range(nr):
        sl = slice(r * c, (r + 1) * c)
        b = b_all[sl, :]
        b_mid = b[c // 2:c // 2 + 1, :]
        b_last = b[c - 1:c, :]
        q = q_ref[sl, :].astype(F32) * scale
        k = k_ref[sl, :].astype(F32)
        v = v_ref[sl, :]
        qs = (q * jnp.exp(b - b_mid)).astype(BF16)
        ks = (k * jnp.exp(b_mid - b)).astype(BF16)
        sc = [lax.dot_general(qs[:, ksl(u)], ks[:, ksl(u)], nt, preferred_element_type=F32) for u in heads]
        o = [jnp.dot(jnp.where(causal, sc[u], 0.0).astype(BF16), v[:, vsl(u)], preferred_element_type=F32)
             for u in heads]
        qd = q * jnp.exp(b)
        for rp in range(r):
            fac = cum[r] - cum[rp + 1] if rp + 1 < r else None
            qx = (qd if fac is None else qd * jnp.exp(fac)).astype(BF16)
            sx = [lax.dot_general(qx[:, ksl(u)], kds[rp][:, ksl(u)], nt, preferred_element_type=F32)
                  for u in heads]
            o = [o[u] + jnp.dot(sx[u].astype(BF16), vs[rp][:, vsl(u)], preferred_element_type=F32)
                 for u in heads]
        qds.append(qd if cum[r] is None else qd * jnp.exp(cum[r]))
        kd = k * jnp.exp(b_last - b)
        kd32.append(kd)
        kds.append(kd.astype(BF16))
        vs.append(v)
        o_parts.append(o)
        cum.append(b_last if cum[r] is None else cum[r] + b_last)

    q_all = jnp.concatenate(qds, axis=0).astype(BF16) if nr > 1 else qds[0].astype(BF16)
    total = cum[nr]
    k_parts = []
    for r in range(nr):
        if r + 1 < nr:
            k_parts.append((kd32[r] * jnp.exp(total - cum[r + 1])).astype(BF16))
        else:
            k_parts.append(kds[r])
    k_all = jnp.concatenate(k_parts, axis=0) if nr > 1 else k_parts[0]
    dec = jnp.exp(total)
    wblk = 2 * LANES
    dec_w = []
    for u in heads:
        dec_col = jnp.transpose(jnp.broadcast_to(dec[:, ksl(u)], (LANES, dk)))
        dec_w.append(jnp.concatenate([dec_col] * (wblk // LANES), axis=1))
    o_inter = [[] for _ in heads]
    for j in range(dv // wblk):
        for u in heads:
            sl = slice(j * wblk, (j + 1) * wblk)
            vcol = slice(u * dv + j * wblk, u * dv + (j + 1) * wblk)
            s_blk = s_scr[u, :, sl]
            o_inter[u].append(jnp.dot(q_all[:, ksl(u)], s_blk.astype(BF16), preferred_element_type=F32))
            upd = lax.dot_general(k_all[:, ksl(u)], v_ref[:, vcol], tn, preferred_element_type=F32)
            s_scr[u, :, sl] = s_blk * dec_w[u] + upd

    g = g_ref[...]
    for r in range(nr):
        sl = slice(r * c, (r + 1) * c)
        for u in heads:
            o = o_parts[r][u] + jnp.concatenate([p[sl, :] for p in o_inter[u]], axis=1)
            on = o * lax.rsqrt(jnp.mean(o * o, axis=-1, keepdims=True) + NORM_EPS) * g
            o_ref[sl, vsl(u)] = (on * _silu(ga_ref[sl, vsl(u)].astype(F32))).astype(o_ref.dtype)

    @pl.when(n == n_steps - 1)
    def _():
        sout_ref[...] = s_scr[...]


def _gla_call(proj, ra3, wgk3, b_gk, g_gla_out, s0, batch, seq):
    h = GLA_HEADS
    hp = GLA_HEADS_PER_STEP
    hg = h // hp
    dk = wgk3.shape[1] // h
    dv = g_gla_out.shape[0]
    c = min(CHUNK, seq)
    rows = c * GLA_BLOCK_CHUNKS if seq % (c * GLA_BLOCK_CHUNKS) == 0 else c
    ns = seq // rows
    t = batch * seq
    wk, wv = hp * dk, hp * dv
    k_off = (h * dk) // wk
    v_off = (2 * h * dk) // wv
    g_off = v_off + hg
    row = lambda b, hh, n: b * ns + n
    kern = functools.partial(_gla_kernel, n_steps=ns, c=c, scale=dk ** -0.5, hp=hp)
    return pl.pallas_call(
        kern,
        grid=(batch, hg, ns),
        in_specs=[
            pl.BlockSpec((rows, wk), lambda b, hh, n: (row(b, hh, n), hh)),
            pl.BlockSpec((rows, wk), lambda b, hh, n: (row(b, hh, n), k_off + hh)),
            pl.BlockSpec((rows, wv), lambda b, hh, n: (row(b, hh, n), v_off + hh)),
            pl.BlockSpec((rows, wv), lambda b, hh, n: (row(b, hh, n), g_off + hh)),
            pl.BlockSpec((rows, LANES), lambda b, hh, n: (row(b, hh, n), 0)),
            pl.BlockSpec((LANES, wk), lambda b, hh, n: (0, hh)),
            pl.BlockSpec((1, wk), lambda b, hh, n: (0, hh)),
            pl.BlockSpec((1, dv), lambda b, hh, n: (0, 0)),
            pl.BlockSpec((None, hp, dk, dv), lambda b, hh, n: (b, hh, 0, 0)),
        ],
        out_specs=[
            pl.BlockSpec((rows, wv), lambda b, hh, n: (row(b, hh, n), hh)),
            pl.BlockSpec((None, hp, dk, dv), lambda b, hh, n: (b, hh, 0, 0)),
        ],
        out_shape=[jax.ShapeDtypeStruct((t, h * dv), BF16),
                   jax.ShapeDtypeStruct((batch, h, dk, dv), F32)],
        scratch_shapes=[pltpu.VMEM((hp, dk, dv), F32)],
        compiler_params=_cparams(("parallel", "parallel", "arbitrary")),
        name="gla_chunked",
    )(proj, proj, proj, proj, ra3, wgk3, b_gk.reshape(1, -1), g_gla_out.reshape(1, -1), s0)


def _dup_half(x, half, lane_lo):
    rolled = pltpu.roll(x, 64, 1)
    return jnp.where(lane_lo == (half == 0), x, rolled)


def _swa_kernel(*refs, tq, n_ka, masked):
    bias_ref, q_ref, cos_ref, sin_ref = refs[0:4]
    ka_refs = refs[4:4 + n_ka]
    va_refs = refs[4 + n_ka:4 + 2 * n_ka]
    kb_ref, vb_ref, oa_ref, za_ref, zb_ref, o_ref = refs[4 + 2 * n_ka:]

    hd = LANES // 2
    grows = SWA_GROUP * tq
    fh = _first_half_mask(tq)
    cos_t, sin_t = cos_ref[...], sin_ref[...]
    q_scale = hd ** -0.5 * LOG2E

    lane_q = lax.broadcasted_iota(jnp.int32, (tq, LANES), 1) < hd
    lane_a = lax.broadcasted_iota(jnp.int32, (WINDOW, LANES), 1) < hd
    ones = jnp.ones((2 * LANES, LANES), BF16)
    zpad = jnp.zeros((LANES - tq, LANES), BF16)

    if masked:
        n = pl.program_id(1)
        valid_a = lax.broadcasted_iota(jnp.int32, (grows, LANES), 1) >= (WINDOW - n * tq)

    nt = (((1,), (1,)), ((), ()))

    def keys_vals(h):
        pc, half = h // 2, h % 2
        sl = slice(pc * LANES, (pc + 1) * LANES)
        ka = jnp.concatenate([r[:, sl] for r in ka_refs], axis=0) if n_ka > 1 else ka_refs[0][:, sl]
        va = jnp.concatenate([r[:, sl] for r in va_refs], axis=0) if n_ka > 1 else va_refs[0][:, sl]
        k_all = jnp.concatenate([_dup_half(ka.astype(F32), half, lane_a).astype(BF16),
                                 _dup_half(kb_ref[:, sl].astype(F32), half, lane_q).astype(BF16), zpad], axis=0)
        v_all = jnp.concatenate([_dup_half(va.astype(F32), half, lane_a).astype(BF16),
                                 _dup_half(vb_ref[:, sl].astype(F32), half, lane_q).astype(BF16), zpad], axis=0)
        return k_all, v_all

    def queries(h):
        rows = []
        for g in range(SWA_GROUP):
            cb = h * (SWA_GROUP // 2) + g // 2
            qsl = slice(cb * LANES, (cb + 1) * LANES)
            qr = _rope_block(q_ref[:, qsl].astype(F32), cos_t, sin_t, fh) * q_scale
            rows.append(jnp.where(lane_q == (g % 2 == 0), qr, 0.0).astype(BF16))
        return jnp.concatenate(rows, axis=0)

    def finish(h, out2):
        for j in range(SWA_GROUP // 2):
            cb = h * (SWA_GROUP // 2) + j
            osl = slice(cb * LANES, (cb + 1) * LANES)
            ob = jnp.where(lane_q, out2[(2 * j) * tq:(2 * j + 1) * tq, :],
                           out2[(2 * j + 1) * tq:(2 * j + 2) * tq, :])
            merged = (_sigmoid(za_ref[:, osl].astype(F32)) * oa_ref[:, osl].astype(F32)
                      + _sigmoid(zb_ref[:, osl].astype(F32)) * ob)
            o_ref[:, osl] = merged.astype(o_ref.dtype)

    hs_per = SWA_HEADS_PER_STAGE
    for h0 in range(0, SWA_KV_HEADS, hs_per):
        hs = range(h0, h0 + hs_per)
        kv = [keys_vals(h) for h in hs]
        lhs = [queries(h) for h in hs]
        logits = [lax.dot_general(lhs[i], kv[i][0], nt, preferred_element_type=F32) for i in range(hs_per)]
        l1 = [jnp.where(valid_a, lg[:, :LANES], -jnp.inf) if masked else lg[:, :LANES] for lg in logits]
        l2 = [logits[i][:, LANES:] + bias_ref[h0 + i] for i in range(hs_per)]
        m = [jnp.max(jnp.maximum(l1[i], l2[i]), axis=-1, keepdims=True) for i in range(hs_per)]
        p = [jnp.concatenate([jnp.exp2(l1[i] - m[i]), jnp.exp2(l2[i] - m[i])], axis=1).astype(BF16)
             for i in range(hs_per)]
        pv = [jnp.dot(p[i], kv[i][1], preferred_element_type=F32) for i in range(hs_per)]
        den = [jnp.dot(p[i], ones, preferred_element_type=F32) for i in range(hs_per)]
        for i in range(hs_per):
            finish(h0 + i, pv[i] / den[i])


def _swa_call(proj, q_blk, k_rot, o_a, sinks, cos_t, sin_t, batch, seq, cache_k=None, cache_v=None):
    t, d = o_a.shape
    kvw = k_rot.shape[1]
    v_blk = ((q_blk + 3) * d) // kvw + 1
    if cache_k is None:
        tq = CHUNK
        nq = seq // tq
        rowi = lambda b, n: b * nq + n
        prev = lambda b, n, back: b * nq + jnp.maximum(n - back, 0)
        n_ka = WINDOW // CHUNK
        ka_args = [k_rot] * n_ka
        va_args = [proj] * n_ka
        ka_specs = [pl.BlockSpec((tq, kvw), functools.partial(lambda b, n, back: (prev(b, n, back), 0),
                                                               back=n_ka - i)) for i in range(n_ka)]
        va_specs = [pl.BlockSpec((tq, kvw), functools.partial(lambda b, n, back: (prev(b, n, back), v_blk),
                                                               back=n_ka - i)) for i in range(n_ka)]
        masked = True
    else:
        tq = seq
        nq = 1
        rowi = lambda b, n: b
        n_ka = 1
        ka_args, va_args = [cache_k], [cache_v]
        ka_specs = [pl.BlockSpec((None, WINDOW, kvw), lambda b, n: (b, 0, 0))]
        va_specs = [pl.BlockSpec((None, WINDOW, kvw), lambda b, n: (b, 0, 0))]
        masked = False
    wide = lambda blk: pl.BlockSpec((tq, d), lambda b, n: (rowi(b, n), blk))
    tab = pl.BlockSpec((tq, LANES), lambda b, n: (n, 0))
    lane = jnp.arange(LANES)
    sink_l2 = (sinks.astype(F32) * LOG2E).reshape(SWA_KV_HEADS, SWA_GROUP, 1, 1)
    bias_row = jnp.where(lane < tq, 0.0, jnp.where(lane == tq, sink_l2, -jnp.inf))
    sinks = jnp.broadcast_to(bias_row, (SWA_KV_HEADS, SWA_GROUP, tq, LANES)).reshape(
        SWA_KV_HEADS, SWA_GROUP * tq, LANES)
    in_specs = ([pl.BlockSpec(sinks.shape, lambda b, n: (0, 0, 0)), wide(q_blk), tab, tab] + ka_specs + va_specs +
                [pl.BlockSpec((tq, kvw), lambda b, n: (rowi(b, n), 0)),
                 pl.BlockSpec((tq, kvw), lambda b, n: (rowi(b, n), v_blk)),
                 wide(0), wide(q_blk + 1), wide(q_blk + 2)])
    args = [sinks, proj, cos_t, sin_t] + ka_args + va_args + [k_rot, proj, o_a, proj, proj]
    return pl.pallas_call(
        functools.partial(_swa_kernel, tq=tq, n_ka=n_ka, masked=masked),
        grid=(batch, nq),
        in_specs=in_specs,
        out_specs=pl.BlockSpec((tq, d), lambda b, n: (rowi(b, n), 0)),
        out_shape=jax.ShapeDtypeStruct((t, d), BF16),
        compiler_params=_cparams(("parallel", "arbitrary")),
        name="swa_merge",
    )(*args)


def _prep_weights(w_in, w_out, w_gate_up, w_down, w_gk_up, d):
    h = GLA_HEADS
    key = (d // (2 * h)) * h
    n1 = 2 * key + 2 * d
    kvw = SWA_KV_HEADS * (d // SWA_HEADS)
    w = kvw
    nb1 = n1 // w
    per_d = d // w
    q_s = list(range(per_d))
    k_s, v_s = per_d, per_d + 1
    za_s = [per_d + 2 + i for i in range(per_d)]
    zb_s = [2 * per_d + 2 + i for i in range(per_d)]
    shifted = q_s + za_s + zb_s + [k_s, v_s]
    src_blk = list(range(nb1)) + [nb1 + s for s in shifted]
    mode = [0] * nb1 + [1] * len(shifted)
    w12 = _relayout_call(w_in, w, src_blk, mode, shift=GLA_GATE_RANK, tr=w_in.shape[0] // 2)

    ra_cols = w_in[:, n1:n1 + GLA_GATE_RANK]
    w_r = jnp.pad(jnp.concatenate([ra_cols] * 3, axis=1),
                  ((0, 0), (0, LANES - 3 * GLA_GATE_RANK))).astype(BF16)
    wg_hi = w_gk_up.astype(BF16)
    wg_lo = (w_gk_up - wg_hi.astype(F32)).astype(BF16)
    wgk3 = jnp.pad(jnp.concatenate([wg_hi, wg_hi, wg_lo], axis=0),
                   ((0, LANES - 3 * GLA_GATE_RANK), (0, 0)))

    w_o = _relayout_call(w_out, 256, list(range(w_out.shape[1] // 256)), [0] * (w_out.shape[1] // 256))

    dff = w_down.shape[0]
    sub = 256
    n_src = dff // sub
    n_tiles = -(-dff // FF_TILE)
    per_tile = FF_TILE // sub
    src_blk, mode = [], []
    for j in range(n_tiles):
        for part in range(2):
            for s in range(per_tile):
                blk = j * per_tile + s
                ok = blk < n_src
                src_blk.append(part * n_src + blk if ok else 0)
                mode.append(0 if ok else 2)
    w_gu = _relayout_call(w_gate_up, sub, src_blk, mode)
    w_dn = _relayout_call(w_down, 256, list(range(d // 256)), [0] * (d // 256), tr=dff // 2)
    return w12, w_r, wgk3, w_o, w_gu, w_dn


def _layer(x, mod, pos, s0, cache_k, cache_v, weights, params, batch, seq):
    w12, w_r, wgk3, w_o, w_gu, w_dn = weights
    g_norm1, b_gk, g_gla_out, sinks, g_norm2 = params
    t, d = x.shape
    sh1, sc1, gt1, sh2, sc2, gt2 = [m.reshape(batch, 1, d) for m in jnp.split(mod, 6, axis=-1)]
    kvw = SWA_KV_HEADS * (d // SWA_HEADS)
    hd = d // SWA_HEADS
    tm = min(1024, t)
    q_blk = (w12.shape[1] - 3 * d - 2 * kvw) // d

    h = _norm_call(x, g_norm1, sc1, sh1, seq, BF16)
    proj = _mm_call(h, w12, tm=tm, tn=1024, out_dtype=BF16, name="in_proj")
    ra3 = _mm_call(h, w_r, tm=tm, tn=LANES, name="in_proj_rank")

    o_a, s_new = _gla_call(proj, ra3, wgk3, b_gk, g_gla_out, s0, batch, seq)

    cos_t, sin_t = _rope_tables(pos, hd)
    k_blk = ((q_blk + 3) * d) // kvw
    k_rot = _rope_k_call(proj, k_blk, kvw, cos_t, sin_t, seq)
    merged = _swa_call(proj, q_blk, k_rot, o_a, sinks, cos_t, sin_t, batch, seq, cache_k, cache_v)

    x1 = _mm_call(merged, w_o, tm=tm, tn=512, epilogue="residual", res=x, gate=gt1,
                  rows_per_group=seq, name="out_proj")
    h2 = _norm_call(x1, g_norm2, sc2, sh2, seq, BF16)
    act = _mm_call(h2, w_gu, tm=tm, tn=2 * FF_TILE, epilogue="swiglu", out_dtype=BF16, name="ffn_gate_up")
    x2 = _mm_call(act, w_dn, tm=min(512, t), tn=512, epilogue="residual", res=x1, gate=gt2,
                  rows_per_group=seq, name="ffn_down")

    keep = WINDOW if cache_k is None else seq
    k_keep = k_rot.reshape(batch, seq, SWA_KV_HEADS, hd)[:, seq - keep:]
    v_cols = slice((k_blk + 1) * kvw, (k_blk + 2) * kvw)
    v_new = jnp.stack([proj[(b + 1) * seq - keep:(b + 1) * seq, v_cols] for b in range(batch)])
    v_keep = v_new.astype(F32).reshape(batch, keep, SWA_KV_HEADS, hd)
    return x2, s_new, k_keep, v_keep


def kernel(x_prompt, x_sample, state_gla, cache_swa_k, cache_swa_v, c_prompt, c_sample, w_ada, b_ada,
           g_norm1, w_in, w_gk_up, b_gk, g_gla_out, swa_sinks, w_out, g_norm2, w_gate_up, w_down, g_final):
    bp, sp, d = x_prompt.shape
    bs, ss, _ = x_sample.shape
    depth = w_ada.shape[0]
    dk = w_gk_up.shape[2] // GLA_HEADS
    dv = g_gla_out.shape[1]
    kvw = SWA_KV_HEADS * (d // SWA_HEADS)

    pos_p = jnp.arange(sp)
    pos_s = PAST_LEN + jnp.arange(ss)
    s0_p = jnp.zeros((bp, GLA_HEADS, dk, dv), F32)

    nc = bp + bs
    nc_pad = -(-nc // 16) * 16
    c_all = jnp.concatenate([c_prompt, c_sample, jnp.zeros((nc_pad - nc, d), F32)], axis=0)

    hp = x_prompt.reshape(bp * sp, d)
    hs = x_sample.reshape(bs * ss, d)
    outs = [[] for _ in range(6)]
    for l in range(depth):
        mod = _mod_call(c_all, w_ada[l], b_ada[l])
        weights = _prep_weights(w_in[l], w_out[l], w_gate_up[l], w_down[l], w_gk_up[l], d)
        params = (g_norm1[l], b_gk[l], g_gla_out[l], swa_sinks[l], g_norm2[l])
        hp, s_p, k_p, v_p = _layer(hp, mod[:bp], pos_p, s0_p, None, None, weights, params, bp, sp)
        hs, s_s, k_s, v_s = _layer(hs, mod[bp:nc], pos_s, state_gla[l],
                                   cache_swa_k[l].reshape(bs, WINDOW, kvw),
                                   cache_swa_v[l].reshape(bs, WINDOW, kvw), weights, params, bs, ss)
        for lst, val in zip(outs, (s_p, k_p, v_p, s_s, k_s, v_s)):
            lst.append(val)
    y_prompt = _norm_call(hp, g_final, None, None, sp, F32).reshape(bp, sp, d)
    y_sample = _norm_call(hs, g_final, None, None, ss, F32).reshape(bs, ss, d)
    return (y_prompt, y_sample) + tuple(jnp.stack(o) for o in outs)
```

```python
import functools

import jax
import jax.numpy as jnp
import numpy as np
from jax import lax
from jax.experimental import pallas as pl
from jax.experimental.pallas import tpu as pltpu

F32 = jnp.float32
BF16 = jnp.bfloat16

CHUNK = 64
GLA_HEADS = 4
GLA_GATE_RANK = 16
GLA_GATE_NORM = 16.0
SWA_HEADS = 64
SWA_KV_HEADS = 8
SWA_GROUP = SWA_HEADS // SWA_KV_HEADS
WINDOW = 128
ROPE_THETA = 10000.0
NORM_EPS = 1e-6
PAST_LEN = 2048
LOG2E = 1.4426950408889634

LANES = 128
VMEM_LIMIT_BYTES = 56 * 1024 * 1024
GLA_HEADS_PER_STEP = 4
GLA_BLOCK_CHUNKS = 2
SWA_HEADS_PER_STAGE = 4
FF_TILE = 512


def _cparams(sem):
    return pltpu.CompilerParams(dimension_semantics=sem, vmem_limit_bytes=VMEM_LIMIT_BYTES)


def _silu(x):
    return x * (1.0 / (1.0 + jnp.exp(-x)))


def _sigmoid(x):
    return 1.0 / (1.0 + jnp.exp(-x))


def _pick(n, prefs):
    for p in prefs:
        if n % p == 0:
            return p
    return n


def _relayout_kernel(src_ref, mode_ref, main, o_ref):
    mode = mode_ref[pl.program_id(1)]

    @pl.when(mode == 0)
    def _():
        o_ref[...] = main[...].astype(BF16)

    @pl.when(mode == 2)
    def _():
        o_ref[...] = jnp.zeros(o_ref.shape, BF16)


def _relayout_call(src, width, src_blk, mode, tr=None):
    r = src.shape[0]
    tr = r if tr is None else tr
    nb = len(src_blk)
    grid_spec = pltpu.PrefetchScalarGridSpec(
        num_scalar_prefetch=2,
        grid=(r // tr, nb),
        in_specs=[pl.BlockSpec((tr, width), lambda i, b, s, m: (i, s[b]))],
        out_specs=pl.BlockSpec((tr, width), lambda i, b, s, m: (i, b)),
    )
    return pl.pallas_call(
        _relayout_kernel,
        grid_spec=grid_spec,
        out_shape=jax.ShapeDtypeStruct((r, nb * width), BF16),
        compiler_params=_cparams(("parallel", "arbitrary")),
        name="weight_relayout",
    )(jnp.asarray(np.asarray(src_blk, np.int32)), jnp.asarray(np.asarray(mode, np.int32)), src)


def _relayout_t_kernel(src_ref, nxt_ref, mode_ref, main, nxt, o_ref, *, shift):
    mode = mode_ref[pl.program_id(1)]
    w = o_ref.shape[1]

    @pl.when(mode == 0)
    def _():
        o_ref[...] = jnp.transpose(main[...]).astype(BF16)

    @pl.when(mode == 1)
    def _():
        x = jnp.concatenate([main[...], nxt[...]], axis=0)
        o_ref[...] = jnp.transpose(x[shift:shift + w, :]).astype(BF16)


def _relayout_t_call(src_t, width, src_blk, mode, shift, tk):
    r = src_t.shape[1]
    nb = len(src_blk)
    src_blk = np.asarray(src_blk, np.int32)
    nxt_blk = (src_blk + 1) * (width // shift)
    grid_spec = pltpu.PrefetchScalarGridSpec(
        num_scalar_prefetch=3,
        grid=(r // tk, nb),
        in_specs=[pl.BlockSpec((width, tk), lambda i, b, s, nx, m: (s[b], i)),
                  pl.BlockSpec((shift, tk), lambda i, b, s, nx, m: (nx[b], i))],
        out_specs=pl.BlockSpec((tk, width), lambda i, b, s, nx, m: (i, b)),
    )
    return pl.pallas_call(
        functools.partial(_relayout_t_kernel, shift=shift),
        grid_spec=grid_spec,
        out_shape=jax.ShapeDtypeStruct((r, nb * width), BF16),
        compiler_params=_cparams(("parallel", "arbitrary")),
        name="weight_relayout_t",
    )(jnp.asarray(src_blk), jnp.asarray(nxt_blk, jnp.int32), jnp.asarray(np.asarray(mode, np.int32)),
      src_t, src_t)


def _mod_kernel(c_ref, w_ref, b_ref, o_ref):
    a = _silu(c_ref[...]).astype(BF16)
    w = w_ref[...].astype(BF16)
    o_ref[...] = jnp.dot(a, w, preferred_element_type=F32) + b_ref[...]


def _mod_call(c_all, w_ada, b_ada):
    m, d = c_all.shape
    n = w_ada.shape[1]
    tn = _pick(n, (512, 256, 128))
    return pl.pallas_call(
        _mod_kernel,
        grid=(n // tn,),
        in_specs=[pl.BlockSpec((m, d), lambda j: (0, 0)),
                  pl.BlockSpec((d, tn), lambda j: (0, j)),
                  pl.BlockSpec((1, tn), lambda j: (0, j))],
        out_specs=pl.BlockSpec((m, tn), lambda j: (0, j)),
        out_shape=jax.ShapeDtypeStruct((m, n), F32),
        compiler_params=_cparams(("arbitrary",)),
        name="adaln_mod",
    )(c_all, w_ada, b_ada.reshape(1, n))


def _norm_kernel(x_ref, g_ref, *rest, modulate):
    o_ref = rest[-1]
    x = x_ref[...]
    y = x * lax.rsqrt(jnp.mean(x * x, axis=-1, keepdims=True) + NORM_EPS) * g_ref[...]
    if modulate:
        sc_ref, sh_ref = rest[0], rest[1]
        y = y * (1.0 + sc_ref[...]) + sh_ref[...]
    o_ref[...] = y.astype(o_ref.dtype)


def _norm_call(x, g, sc, sh, rows_per_group, out_dtype):
    t, d = x.shape
    tr = _pick(rows_per_group, (256, 128, 64, 32, 16, 8))
    tiles = rows_per_group // tr
    modulate = sc is not None
    in_specs = [pl.BlockSpec((tr, d), lambda i: (i, 0)), pl.BlockSpec((1, d), lambda i: (0, 0))]
    args = [x, g.reshape(1, d)]
    if modulate:
        in_specs += [pl.BlockSpec((None, 1, d), lambda i: (i // tiles, 0, 0))] * 2
        args += [sc, sh]
    return pl.pallas_call(
        functools.partial(_norm_kernel, modulate=modulate),
        grid=(t // tr,),
        in_specs=in_specs,
        out_specs=pl.BlockSpec((tr, d), lambda i: (i, 0)),
        out_shape=jax.ShapeDtypeStruct((t, d), out_dtype),
        compiler_params=_cparams(("parallel",)),
        name="rmsnorm_mod",
    )(*args)


def _mm_kernel(a_ref, w_ref, *rest, nk, epilogue):
    if nk > 1:
        acc_ref = rest[-1]
        rest = rest[:-1]
    o_ref = rest[-1]
    extras = rest[:-1]

    def finish(acc):
        if epilogue == "residual":
            res_ref, gate_ref = extras
            out = res_ref[...] + gate_ref[...] * acc
        elif epilogue == "swiglu":
            half = acc.shape[1] // 2
            out = _silu(acc[:, :half]) * acc[:, half:]
        else:
            out = acc
        o_ref[...] = out.astype(o_ref.dtype)

    part = jnp.dot(a_ref[...], w_ref[...], preferred_element_type=F32)
    if nk == 1:
        finish(part)
    else:
        k = pl.program_id(2)

        @pl.when(k == 0)
        def _():
            acc_ref[...] = part

        @pl.when(jnp.logical_and(k > 0, k < nk - 1))
        def _():
            acc_ref[...] += part

        @pl.when(k == nk - 1)
        def _():
            finish(acc_ref[...] + part)


def _mm_call(a, w, *, tm, tn, tk=None, epilogue="plain", res=None, gate=None, rows_per_group=None,
             out_dtype=F32, name="matmul"):
    m = a.shape[0]
    kdim, n = w.shape
    tk = kdim if tk is None else tk
    nk = kdim // tk
    n_out = n // 2 if epilogue == "swiglu" else n
    tn_out = tn // 2 if epilogue == "swiglu" else tn
    in_specs = [pl.BlockSpec((tm, tk), lambda i, j, k: (i, k)),
                pl.BlockSpec((tk, tn), lambda i, j, k: (k, j))]
    args = [a, w]
    if epilogue == "residual":
        in_specs.append(pl.BlockSpec((tm, tn), lambda i, j, k: (i, j)))
        if rows_per_group % tm == 0:
            tiles = rows_per_group // tm
            in_specs.append(pl.BlockSpec((None, 1, tn), lambda i, j, k: (i // tiles, 0, j)))
        else:
            gate = jnp.repeat(gate[:, 0, :], rows_per_group, axis=0)
            in_specs.append(pl.BlockSpec((tm, tn), lambda i, j, k: (i, j)))
        args += [res, gate]
    scratch = [pltpu.VMEM((tm, tn), F32)] if nk > 1 else []
    return pl.pallas_call(
        functools.partial(_mm_kernel, nk=nk, epilogue=epilogue),
        grid=(m // tm, n // tn, nk),
        in_specs=in_specs,
        out_specs=pl.BlockSpec((tm, tn_out), lambda i, j, k: (i, j)),
        out_shape=jax.ShapeDtypeStruct((m, n_out), out_dtype),
        scratch_shapes=scratch,
        compiler_params=_cparams(("parallel", "parallel", "arbitrary")),
        name=name,
    )(*args)


def _rope_tables(pos, head_dim):
    half = head_dim // 2
    inv = ROPE_THETA ** (-jnp.arange(half, dtype=F32) / half)
    ang = pos.astype(F32)[:, None] * inv[None, :]
    cos, sin = jnp.cos(ang), jnp.sin(ang)
    reps = LANES // head_dim
    cos_t = jnp.tile(jnp.concatenate([cos, cos], axis=1), (1, reps))
    sin_t = jnp.tile(jnp.concatenate([-sin, sin], axis=1), (1, reps))
    return cos_t, sin_t


def _rope_block(x, cos_t, sin_t, first_half):
    partner = jnp.where(first_half, pltpu.roll(x, 96, 1), pltpu.roll(x, 32, 1))
    return x * cos_t + partner * sin_t


def _first_half_mask(rows):
    lane = lax.broadcasted_iota(jnp.int32, (rows, LANES), 1)
    return (lane % 64) < 32


def _rope_k_kernel(k_ref, cos_ref, sin_ref, o_ref):
    rows, width = k_ref.shape
    fh = _first_half_mask(rows)
    cos_t, sin_t = cos_ref[...], sin_ref[...]
    for c in range(width // LANES):
        sl = slice(c * LANES, (c + 1) * LANES)
        o_ref[:, sl] = _rope_block(k_ref[:, sl].astype(F32), cos_t, sin_t, fh)


def _rope_k_call(proj, col_block, width, cos_t, sin_t, rows_per_group):
    t = proj.shape[0]
    tr = _pick(rows_per_group, (256, 128, 64, 32, 16, 8))
    tiles = rows_per_group // tr
    return pl.pallas_call(
        _rope_k_kernel,
        grid=(t // tr,),
        in_specs=[pl.BlockSpec((tr, width), lambda i: (i, col_block)),
                  pl.BlockSpec((tr, LANES), lambda i: (i % tiles, 0)),
                  pl.BlockSpec((tr, LANES), lambda i: (i % tiles, 0))],
        out_specs=pl.BlockSpec((tr, width), lambda i: (i, 0)),
        out_shape=jax.ShapeDtypeStruct((t, width), F32),
        compiler_params=_cparams(("parallel",)),
        name="rope_k",
    )(proj, cos_t, sin_t)


def _gla_kernel(q_ref, k_ref, v_ref, ga_ref, ra_ref, wgk_ref, bgk_ref, g_ref, s0_ref,
                o_ref, sout_ref, s_scr, *, n_steps, c, scale, hp):
    n = pl.program_id(2)
    rows = q_ref.shape[0]
    dk = q_ref.shape[1] // hp
    dv = v_ref.shape[1] // hp
    nr = rows // c
    nt = (((1,), (1,)), ((), ()))
    tn = (((0,), (0,)), ((), ()))
    heads = range(hp)
    ksl = lambda u: slice(u * dk, (u + 1) * dk)
    vsl = lambda u: slice(u * dv, (u + 1) * dv)

    @pl.when(n == 0)
    def _():
        s_scr[...] = s0_ref[...]

    ra = ra_ref[...]
    lane = lax.broadcasted_iota(jnp.int32, ra.shape, 1)
    ra_hi = ra.astype(BF16)
    ra_lo = (ra - ra_hi.astype(F32)).astype(BF16)
    lhs = jnp.where(jnp.logical_and(lane >= GLA_GATE_RANK, lane < 2 * GLA_GATE_RANK), ra_lo, ra_hi)
    z = jnp.dot(lhs, wgk_ref[...], preferred_element_type=F32) + bgk_ref[...]
    log_a = (jnp.minimum(z, 0.0) - jnp.log(1.0 + jnp.exp(-jnp.abs(z)))) * (1.0 / GLA_GATE_NORM)

    ri = lax.broadcasted_iota(jnp.int32, (rows, 2 * rows), 0)
    ci = lax.broadcasted_iota(jnp.int32, (rows, 2 * rows), 1) % rows
    tril2 = jnp.where(jnp.logical_and(ri // c == ci // c, ci <= ri), 1.0, 0.0).astype(BF16)
    la_hi = log_a.astype(BF16)
    la_lo = (log_a - la_hi.astype(F32)).astype(BF16)
    b_all = jnp.dot(tril2, jnp.concatenate([la_hi, la_lo], axis=0), preferred_element_type=F32)

    row = lax.broadcasted_iota(jnp.int32, (c, c), 0)
    col = lax.broadcasted_iota(jnp.int32, (c, c), 1)
    causal = row >= col

    vs, qds, kds, kd32, o_parts = [], [], [], [], []
    cum = [None]
    for r in range(nr):
        sl = slice(r * c, (r + 1) * c)
        b = b_all[sl, :]
        b_mid = b[c // 2:c // 2 + 1, :]
        b_last = b[c - 1:c, :]
        q = q_ref[sl, :].astype(F32) * scale
        k = k_ref[sl, :].astype(F32)
        v = v_ref[sl, :]
        qs = (q * jnp.exp(b - b_mid)).astype(BF16)
        ks = (k * jnp.exp(b_mid - b)).astype(BF16)
        sc = [lax.dot_general(qs[:, ksl(u)], ks[:, ksl(u)], nt, preferred_element_type=F32) for u in heads]
        o = [jnp.dot(jnp.where(causal, sc[u], 0.0).astype(BF16), v[:, vsl(u)], preferred_element_type=F32)
             for u in heads]
        qd = q * jnp.exp(b)
        for rp in range(r):
            fac = cum[r] - cum[rp + 1] if rp + 1 < r else None
            qx = (qd if fac is None else qd * jnp.exp(fac)).astype(BF16)
            sx = [lax.dot_general(qx[:, ksl(u)], kds[rp][:, ksl(u)], nt, preferred_element_type=F32)
                  for u in heads]
            o = [o[u] + jnp.dot(sx[u].astype(BF16), vs[rp][:, vsl(u)], preferred_element_type=F32)
                 for u in heads]
        qds.append(qd if cum[r] is None else qd * jnp.exp(cum[r]))
        kd = k * jnp.exp(b_last - b)
        kd32.append(kd)
        kds.append(kd.astype(BF16))
        vs.append(v)
        o_parts.append(o)
        cum.append(b_last if cum[r] is None else cum[r] + b_last)

    q_all = jnp.concatenate(qds, axis=0).astype(BF16) if nr > 1 else qds[0].astype(BF16)
    total = cum[nr]
    k_parts = []
    for r in range(nr):
        if r + 1 < nr:
            k_parts.append((kd32[r] * jnp.exp(total - cum[r + 1])).astype(BF16))
        else:
            k_parts.append(kds[r])
    k_all = jnp.concatenate(k_parts, axis=0) if nr > 1 else k_parts[0]
    dec = jnp.exp(total)
    wblk = 2 * LANES
    dec_w = []
    for u in heads:
        dec_col = jnp.transpose(jnp.broadcast_to(dec[:, ksl(u)], (LANES, dk)))
        dec_w.append(jnp.concatenate([dec_col] * (wblk // LANES), axis=1))
    o_inter = [[] for _ in heads]
    for j in range(dv // wblk):
        for u in heads:
            sl = slice(j * wblk, (j + 1) * wblk)
            vcol = slice(u * dv + j * wblk, u * dv + (j + 1) * wblk)
            s_blk = s_scr[u, :, sl]
            o_inter[u].append(jnp.dot(q_all[:, ksl(u)], s_blk.astype(BF16), preferred_element_type=F32))
            upd = lax.dot_general(k_all[:, ksl(u)], v_ref[:, vcol], tn, preferred_element_type=F32)
            s_scr[u, :, sl] = s_blk * dec_w[u] + upd

    g = g_ref[...]
    for r in range(nr):
        sl = slice(r * c, (r + 1) * c)
        for u in heads:
            o = o_parts[r][u] + jnp.concatenate([p[sl, :] for p in o_inter[u]], axis=1)
            on = o * lax.rsqrt(jnp.mean(o * o, axis=-1, keepdims=True) + NORM_EPS) * g
            o_ref[sl, vsl(u)] = (on * _silu(ga_ref[sl, vsl(u)].astype(F32))).astype(o_ref.dtype)

    @pl.when(n == n_steps - 1)
    def _():
        sout_ref[...] = s_scr[...]


def _gla_call(proj, ra3, wgk3, b_gk, g_gla_out, s0, batch, seq):
    h = GLA_HEADS
    hp = GLA_HEADS_PER_STEP
    hg = h // hp
    dk = wgk3.shape[1] // h
    dv = g_gla_out.shape[0]
    c = min(CHUNK, seq)
    rows = c * GLA_BLOCK_CHUNKS if seq % (c * GLA_BLOCK_CHUNKS) == 0 else c
    ns = seq // rows
    t = batch * seq
    wk, wv = hp * dk, hp * dv
    k_off = (h * dk) // wk
    v_off = (2 * h * dk) // wv
    g_off = v_off + hg
    row = lambda b, hh, n: b * ns + n
    kern = functools.partial(_gla_kernel, n_steps=ns, c=c, scale=dk ** -0.5, hp=hp)
    return pl.pallas_call(
        kern,
        grid=(batch, hg, ns),
        in_specs=[
            pl.BlockSpec((rows, wk), lambda b, hh, n: (row(b, hh, n), hh)),
            pl.BlockSpec((rows, wk), lambda b, hh, n: (row(b, hh, n), k_off + hh)),
            pl.BlockSpec((rows, wv), lambda b, hh, n: (row(b, hh, n), v_off + hh)),
            pl.BlockSpec((rows, wv), lambda b, hh, n: (row(b, hh, n), g_off + hh)),
            pl.BlockSpec((rows, LANES), lambda b, hh, n: (row(b, hh, n), 0)),
            pl.BlockSpec((LANES, wk), lambda b, hh, n: (0, hh)),
            pl.BlockSpec((1, wk), lambda b, hh, n: (0, hh)),
            pl.BlockSpec((1, dv), lambda b, hh, n: (0, 0)),
            pl.BlockSpec((None, hp, dk, dv), lambda b, hh, n: (b, hh, 0, 0)),
        ],
        out_specs=[
            pl.BlockSpec((rows, wv), lambda b, hh, n: (row(b, hh, n), hh)),
            pl.BlockSpec((None, hp, dk, dv), lambda b, hh, n: (b, hh, 0, 0)),
        ],
        out_shape=[jax.ShapeDtypeStruct((t, h * dv), BF16),
                   jax.ShapeDtypeStruct((batch, h, dk, dv), F32)],
        scratch_shapes=[pltpu.VMEM((hp, dk, dv), F32)],
        compiler_params=_cparams(("parallel", "parallel", "arbitrary")),
        name="gla_chunked",
    )(proj, proj, proj, proj, ra3, wgk3, b_gk.reshape(1, -1), g_gla_out.reshape(1, -1), s0)


def _dup_half(x, half, lane_lo):
    rolled = pltpu.roll(x, 64, 1)
    return jnp.where(lane_lo == (half == 0), x, rolled)


def _swa_kernel(*refs, tq, n_ka, masked):
    bias_ref, q_ref, cos_ref, sin_ref = refs[0:4]
    ka_refs = refs[4:4 + n_ka]
    va_refs = refs[4 + n_ka:4 + 2 * n_ka]
    kb_ref, vb_ref, oa_ref, za_ref, zb_ref, o_ref = refs[4 + 2 * n_ka:]

    hd = LANES // 2
    grows = SWA_GROUP * tq
    fh = _first_half_mask(tq)
    cos_t, sin_t = cos_ref[...], sin_ref[...]
    q_scale = hd ** -0.5 * LOG2E

    lane_q = lax.broadcasted_iota(jnp.int32, (tq, LANES), 1) < hd
    lane_a = lax.broadcasted_iota(jnp.int32, (WINDOW, LANES), 1) < hd
    ones = jnp.ones((2 * LANES, LANES), BF16)
    zpad = jnp.zeros((LANES - tq, LANES), BF16)

    if masked:
        n = pl.program_id(1)
        valid_a = lax.broadcasted_iota(jnp.int32, (grows, LANES), 1) >= (WINDOW - n * tq)

    nt = (((1,), (1,)), ((), ()))

    def keys_vals(h):
        pc, half = h // 2, h % 2
        sl = slice(pc * LANES, (pc + 1) * LANES)
        ka = jnp.concatenate([r[:, sl] for r in ka_refs], axis=0) if n_ka > 1 else ka_refs[0][:, sl]
        va = jnp.concatenate([r[:, sl] for r in va_refs], axis=0) if n_ka > 1 else va_refs[0][:, sl]
        k_all = jnp.concatenate([_dup_half(ka.astype(F32), half, lane_a).astype(BF16),
                                 _dup_half(kb_ref[:, sl].astype(F32), half, lane_q).astype(BF16), zpad], axis=0)
        v_all = jnp.concatenate([_dup_half(va.astype(F32), half, lane_a).astype(BF16),
                                 _dup_half(vb_ref[:, sl].astype(F32), half, lane_q).astype(BF16), zpad], axis=0)
        return k_all, v_all

    def queries(h):
        rows = []
        for g in range(SWA_GROUP):
            cb = h * (SWA_GROUP // 2) + g // 2
            qsl = slice(cb * LANES, (cb + 1) * LANES)
            qr = _rope_block(q_ref[:, qsl].astype(F32), cos_t, sin_t, fh) * q_scale
            rows.append(jnp.where(lane_q == (g % 2 == 0), qr, 0.0).astype(BF16))
        return jnp.concatenate(rows, axis=0)

    def finish(h, out2):
        for j in range(SWA_GROUP // 2):
            cb = h * (SWA_GROUP // 2) + j
            osl = slice(cb * LANES, (cb + 1) * LANES)
            ob = jnp.where(lane_q, out2[(2 * j) * tq:(2 * j + 1) * tq, :],
                           out2[(2 * j + 1) * tq:(2 * j + 2) * tq, :])
            merged = (_sigmoid(za_ref[:, osl].astype(F32)) * oa_ref[:, osl].astype(F32)
                      + _sigmoid(zb_ref[:, osl].astype(F32)) * ob)
            o_ref[:, osl] = merged.astype(o_ref.dtype)

    hs_per = SWA_HEADS_PER_STAGE
    for h0 in range(0, SWA_KV_HEADS, hs_per):
        hs = range(h0, h0 + hs_per)
        kv = [keys_vals(h) for h in hs]
        lhs = [queries(h) for h in hs]
        logits = [lax.dot_general(lhs[i], kv[i][0], nt, preferred_element_type=F32) for i in range(hs_per)]
        l1 = [jnp.where(valid_a, lg[:, :LANES], -jnp.inf) if masked else lg[:, :LANES] for lg in logits]
        l2 = [logits[i][:, LANES:] + bias_ref[h0 + i] for i in range(hs_per)]
        m = [jnp.max(jnp.maximum(l1[i], l2[i]), axis=-1, keepdims=True) for i in range(hs_per)]
        p = [jnp.concatenate([jnp.exp2(l1[i] - m[i]), jnp.exp2(l2[i] - m[i])], axis=1).astype(BF16)
             for i in range(hs_per)]
        pv = [jnp.dot(p[i], kv[i][1], preferred_element_type=F32) for i in range(hs_per)]
        den = [jnp.dot(p[i], ones, preferred_element_type=F32) for i in range(hs_per)]
        for i in range(hs_per):
            finish(h0 + i, pv[i] / den[i])


def _swa_call(proj, q_blk, k_rot, o_a, sinks, cos_t, sin_t, batch, seq, cache_k=None, cache_v=None):
    t, d = o_a.shape
    kvw = k_rot.shape[1]
    v_blk = ((q_blk + 3) * d) // kvw + 1
    if cache_k is None:
        tq = CHUNK
        nq = seq // tq
        rowi = lambda b, n: b * nq + n
        prev = lambda b, n, back: b * nq + jnp.maximum(n - back, 0)
        n_ka = WINDOW // CHUNK
        ka_args = [k_rot] * n_ka
        va_args = [proj] * n_ka
        ka_specs = [pl.BlockSpec((tq, kvw), functools.partial(lambda b, n, back: (prev(b, n, back), 0),
                                                               back=n_ka - i)) for i in range(n_ka)]
        va_specs = [pl.BlockSpec((tq, kvw), functools.partial(lambda b, n, back: (prev(b, n, back), v_blk),
                                                               back=n_ka - i)) for i in range(n_ka)]
        masked = True
    else:
        tq = seq
        nq = 1
        rowi = lambda b, n: b
        n_ka = 1
        ka_args, va_args = [cache_k], [cache_v]
        ka_specs = [pl.BlockSpec((None, WINDOW, kvw), lambda b, n: (b, 0, 0))]
        va_specs = [pl.BlockSpec((None, WINDOW, kvw), lambda b, n: (b, 0, 0))]
        masked = False
    wide = lambda blk: pl.BlockSpec((tq, d), lambda b, n: (rowi(b, n), blk))
    tab = pl.BlockSpec((tq, LANES), lambda b, n: (n, 0))
    lane = jnp.arange(LANES)
    sink_l2 = (sinks.astype(F32) * LOG2E).reshape(SWA_KV_HEADS, SWA_GROUP, 1, 1)
    bias_row = jnp.where(lane < tq, 0.0, jnp.where(lane == tq, sink_l2, -jnp.inf))
    sinks = jnp.broadcast_to(bias_row, (SWA_KV_HEADS, SWA_GROUP, tq, LANES)).reshape(
        SWA_KV_HEADS, SWA_GROUP * tq, LANES)
    in_specs = ([pl.BlockSpec(sinks.shape, lambda b, n: (0, 0, 0)), wide(q_blk), tab, tab] + ka_specs + va_specs +
                [pl.BlockSpec((tq, kvw), lambda b, n: (rowi(b, n), 0)),
                 pl.BlockSpec((tq, kvw), lambda b, n: (rowi(b, n), v_blk)),
                 wide(0), wide(q_blk + 1), wide(q_blk + 2)])
    args = [sinks, proj, cos_t, sin_t] + ka_args + va_args + [k_rot, proj, o_a, proj, proj]
    return pl.pallas_call(
        functools.partial(_swa_kernel, tq=tq, n_ka=n_ka, masked=masked),
        grid=(batch, nq),
        in_specs=in_specs,
        out_specs=pl.BlockSpec((tq, d), lambda b, n: (rowi(b, n), 0)),
        out_shape=jax.ShapeDtypeStruct((t, d), BF16),
        compiler_params=_cparams(("parallel", "arbitrary")),
        name="swa_merge",
    )(*args)


def _prep_weights(w_in, w_out, w_gate_up, w_down, w_gk_up, d):
    h = GLA_HEADS
    key = (d // (2 * h)) * h
    n1 = 2 * key + 2 * d
    kvw = SWA_KV_HEADS * (d // SWA_HEADS)
    w = kvw
    nb1 = n1 // w
    per_d = d // w
    q_s = list(range(per_d))
    k_s, v_s = per_d, per_d + 1
    za_s = [per_d + 2 + i for i in range(per_d)]
    zb_s = [2 * per_d + 2 + i for i in range(per_d)]
    shifted = q_s + za_s + zb_s + [k_s, v_s]
    src_blk = list(range(nb1)) + [nb1 + s for s in shifted]
    mode = [0] * nb1 + [1] * len(shifted)
    w_in_t = jnp.transpose(w_in)
    w12 = _relayout_t_call(w_in_t, w, src_blk, mode, shift=GLA_GATE_RANK, tk=1024)

    ra_cols = jnp.transpose(w_in_t[n1:n1 + GLA_GATE_RANK, :])
    w_r = jnp.pad(jnp.concatenate([ra_cols] * 3, axis=1),
                  ((0, 0), (0, LANES - 3 * GLA_GATE_RANK))).astype(BF16)
    wg_hi = w_gk_up.astype(BF16)
    wg_lo = (w_gk_up - wg_hi.astype(F32)).astype(BF16)
    wgk3 = jnp.pad(jnp.concatenate([wg_hi, wg_hi, wg_lo], axis=0),
                   ((0, LANES - 3 * GLA_GATE_RANK), (0, 0)))

    w_o = _relayout_call(w_out, 256, list(range(w_out.shape[1] // 256)), [0] * (w_out.shape[1] // 256))

    dff = w_down.shape[0]
    sub = 256
    n_src = dff // sub
    n_tiles = -(-dff // FF_TILE)
    per_tile = FF_TILE // sub
    src_blk, mode = [], []
    for j in range(n_tiles):
        for part in range(2):
            for s in range(per_tile):
                blk = j * per_tile + s
                ok = blk < n_src
                src_blk.append(part * n_src + blk if ok else 0)
                mode.append(0 if ok else 2)
    w_gu = _relayout_call(w_gate_up, sub, src_blk, mode)
    w_dn = _relayout_call(w_down, 256, list(range(d // 256)), [0] * (d // 256), tr=dff // 2)
    return w12, w_r, wgk3, w_o, w_gu, w_dn


def _layer(x, mod, pos, s0, cache_k, cache_v, weights, params, batch, seq):
    w12, w_r, wgk3, w_o, w_gu, w_dn = weights
    g_norm1, b_gk, g_gla_out, sinks, g_norm2 = params
    t, d = x.shape
    sh1, sc1, gt1, sh2, sc2, gt2 = [m.reshape(batch, 1, d) for m in jnp.split(mod, 6, axis=-1)]
    kvw = SWA_KV_HEADS * (d // SWA_HEADS)
    hd = d // SWA_HEADS
    tm = min(1024, t)
    q_blk = (w12.shape[1] - 3 * d - 2 * kvw) // d

    h = _norm_call(x, g_norm1, sc1, sh1, seq, BF16)
    proj = _mm_call(h, w12, tm=tm, tn=1024, out_dtype=BF16, name="in_proj")
    ra3 = _mm_call(h, w_r, tm=tm, tn=LANES, name="in_proj_rank")

    o_a, s_new = _gla_call(proj, ra3, wgk3, b_gk, g_gla_out, s0, batch, seq)

    cos_t, sin_t = _rope_tables(pos, hd)
    k_blk = ((q_blk + 3) * d) // kvw
    k_rot = _rope_k_call(proj, k_blk, kvw, cos_t, sin_t, seq)
    merged = _swa_call(proj, q_blk, k_rot, o_a, sinks, cos_t, sin_t, batch, seq, cache_k, cache_v)

    x1 = _mm_call(merged, w_o, tm=tm, tn=1024, epilogue="residual", res=x, gate=gt1,
                  rows_per_group=seq, name="out_proj")
    h2 = _norm_call(x1, g_norm2, sc2, sh2, seq, BF16)
    act = _mm_call(h2, w_gu, tm=tm, tn=2 * FF_TILE, epilogue="swiglu", out_dtype=BF16, name="ffn_gate_up")
    x2 = _mm_call(act, w_dn, tm=min(512, t), tn=512, epilogue="residual", res=x1, gate=gt2,
                  rows_per_group=seq, name="ffn_down")

    keep = WINDOW if cache_k is None else seq
    k_keep = jnp.stack([k_rot[(b + 1) * seq - keep:(b + 1) * seq] for b in range(batch)]).reshape(
        batch, keep, SWA_KV_HEADS, hd)
    v_cols = slice((k_blk + 1) * kvw, (k_blk + 2) * kvw)
    v_new = jnp.stack([proj[(b + 1) * seq - keep:(b + 1) * seq, v_cols] for b in range(batch)])
    v_keep = v_new.astype(F32).reshape(batch, keep, SWA_KV_HEADS, hd)
    return x2, s_new, k_keep, v_keep


def kernel(x_prompt, x_sample, state_gla, cache_swa_k, cache_swa_v, c_prompt, c_sample, w_ada, b_ada,
           g_norm1, w_in, w_gk_up, b_gk, g_gla_out, swa_sinks, w_out, g_norm2, w_gate_up, w_down, g_final):
    bp, sp, d = x_prompt.shape
    bs, ss, _ = x_sample.shape
    depth = w_ada.shape[0]
    dk = w_gk_up.shape[2] // GLA_HEADS
    dv = g_gla_out.shape[1]
    kvw = SWA_KV_HEADS * (d // SWA_HEADS)

    pos_p = jnp.arange(sp)
    pos_s = PAST_LEN + jnp.arange(ss)
    s0_p = jnp.zeros((bp, GLA_HEADS, dk, dv), F32)

    nc = bp + bs
    nc_pad = -(-nc // 16) * 16
    c_all = jnp.concatenate([c_prompt, c_sample, jnp.zeros((nc_pad - nc, d), F32)], axis=0)

    hp = x_prompt.reshape(bp * sp, d)
    hs = x_sample.reshape(bs * ss, d)
    outs = [[] for _ in range(6)]
    for l in range(depth):
        mod = _mod_call(c_all, w_ada[l], b_ada[l])
        weights = _prep_weights(w_in[l], w_out[l], w_gate_up[l], w_down[l], w_gk_up[l], d)
        params = (g_norm1[l], b_gk[l], g_gla_out[l], swa_sinks[l], g_norm2[l])
        hp, s_p, k_p, v_p = _layer(hp, mod[:bp], pos_p, s0_p, None, None, weights, params, bp, sp)
        hs, s_s, k_s, v_s = _layer(hs, mod[bp:nc], pos_s, state_gla[l],
                                   cache_swa_k[l].reshape(bs, WINDOW, kvw),
                                   cache_swa_v[l].reshape(bs, WINDOW, kvw), weights, params, bs, ss)
        for lst, val in zip(outs, (s_p, k_p, v_p, s_s, k_s, v_s)):
            lst.append(val)
    y_prompt = _norm_call(hp, g_final, None, None, sp, F32).reshape(bp, sp, d)
    y_sample = _norm_call(hs, g_final, None, None, ss, F32).reshape(bs, ss, d)
    return (y_prompt, y_sample) + tuple(jnp.stack(o) for o in outs)
```

```python
import functools

import jax
import jax.numpy as jnp
import numpy as np
from jax import lax
from jax.experimental import pallas as pl
from jax.experimental.pallas import tpu as pltpu

F32 = jnp.float32
BF16 = jnp.bfloat16

CHUNK = 64
GLA_HEADS = 4
GLA_GATE_RANK = 16
GLA_GATE_NORM = 16.0
SWA_HEADS = 64
SWA_KV_HEADS = 8
SWA_GROUP = SWA_HEADS // SWA_KV_HEADS
WINDOW = 128
ROPE_THETA = 10000.0
NORM_EPS = 1e-6
PAST_LEN = 2048
LOG2E = 1.4426950408889634

LANES = 128
VMEM_LIMIT_BYTES = 56 * 1024 * 1024
MM_TILE_M = 1024
GLA_HEADS_PER_STEP = 4
GLA_BLOCK_CHUNKS = 2
SWA_HEADS_PER_STAGE = 4
FF_TILE = 512


def _cparams(sem):
    return pltpu.CompilerParams(dimension_semantics=sem, vmem_limit_bytes=VMEM_LIMIT_BYTES)


def _silu(x):
    return x * (1.0 / (1.0 + jnp.exp(-x)))


def _sigmoid(x):
    return 1.0 / (1.0 + jnp.exp(-x))


def _pick(n, prefs):
    for p in prefs:
        if n % p == 0:
            return p
    return n


def _relayout_kernel(src_ref, mode_ref, main, o_ref):
    mode = mode_ref[pl.program_id(1)]

    @pl.when(mode == 0)
    def _():
        o_ref[...] = main[...].astype(BF16)

    @pl.when(mode == 2)
    def _():
        o_ref[...] = jnp.zeros(o_ref.shape, BF16)


def _relayout_call(src, width, src_blk, mode, tr=None):
    r = src.shape[0]
    tr = r if tr is None else tr
    nb = len(src_blk)
    grid_spec = pltpu.PrefetchScalarGridSpec(
        num_scalar_prefetch=2,
        grid=(r // tr, nb),
        in_specs=[pl.BlockSpec((tr, width), lambda i, b, s, m: (i, s[b]))],
        out_specs=pl.BlockSpec((tr, width), lambda i, b, s, m: (i, b)),
    )
    return pl.pallas_call(
        _relayout_kernel,
        grid_spec=grid_spec,
        out_shape=jax.ShapeDtypeStruct((r, nb * width), BF16),
        compiler_params=_cparams(("parallel", "arbitrary")),
        name="weight_relayout",
    )(jnp.asarray(np.asarray(src_blk, np.int32)), jnp.asarray(np.asarray(mode, np.int32)), src)


def _relayout_t_kernel(src_ref, nxt_ref, mode_ref, main, nxt, o_ref, *, shift):
    mode = mode_ref[pl.program_id(1)]
    w = o_ref.shape[1]

    @pl.when(mode == 0)
    def _():
        o_ref[...] = jnp.transpose(main[...]).astype(BF16)

    @pl.when(mode == 1)
    def _():
        x = jnp.concatenate([main[...], nxt[...]], axis=0)
        o_ref[...] = jnp.transpose(x[shift:shift + w, :]).astype(BF16)


def _relayout_t_call(src_t, width, src_blk, mode, shift, tk):
    r = src_t.shape[1]
    nb = len(src_blk)
    src_blk = np.asarray(src_blk, np.int32)
    nxt_blk = (src_blk + 1) * (width // shift)
    grid_spec = pltpu.PrefetchScalarGridSpec(
        num_scalar_prefetch=3,
        grid=(r // tk, nb),
        in_specs=[pl.BlockSpec((width, tk), lambda i, b, s, nx, m: (s[b], i)),
                  pl.BlockSpec((shift, tk), lambda i, b, s, nx, m: (nx[b], i))],
        out_specs=pl.BlockSpec((tk, width), lambda i, b, s, nx, m: (i, b)),
    )
    return pl.pallas_call(
        functools.partial(_relayout_t_kernel, shift=shift),
        grid_spec=grid_spec,
        out_shape=jax.ShapeDtypeStruct((r, nb * width), BF16),
        compiler_params=_cparams(("parallel", "arbitrary")),
        name="weight_relayout_t",
    )(jnp.asarray(src_blk), jnp.asarray(nxt_blk, jnp.int32), jnp.asarray(np.asarray(mode, np.int32)),
      src_t, src_t)


def _mod_kernel(c_ref, w_ref, b_ref, o_ref):
    a = _silu(c_ref[...]).astype(BF16)
    w = w_ref[...].astype(BF16)
    o_ref[...] = jnp.dot(a, w, preferred_element_type=F32) + b_ref[...]


def _mod_call(c_all, w_ada, b_ada):
    m, d = c_all.shape
    n = w_ada.shape[1]
    tn = _pick(n, (512, 256, 128))
    return pl.pallas_call(
        _mod_kernel,
        grid=(n // tn,),
        in_specs=[pl.BlockSpec((m, d), lambda j: (0, 0)),
                  pl.BlockSpec((d, tn), lambda j: (0, j)),
                  pl.BlockSpec((1, tn), lambda j: (0, j))],
        out_specs=pl.BlockSpec((m, tn), lambda j: (0, j)),
        out_shape=jax.ShapeDtypeStruct((m, n), F32),
        compiler_params=_cparams(("arbitrary",)),
        name="adaln_mod",
    )(c_all, w_ada, b_ada.reshape(1, n))


def _norm_kernel(x_ref, g_ref, *rest, modulate):
    o_ref = rest[-1]
    x = x_ref[...]
    y = x * lax.rsqrt(jnp.mean(x * x, axis=-1, keepdims=True) + NORM_EPS) * g_ref[...]
    if modulate:
        sc_ref, sh_ref = rest[0], rest[1]
        y = y * (1.0 + sc_ref[...]) + sh_ref[...]
    o_ref[...] = y.astype(o_ref.dtype)


def _norm_call(x, g, sc, sh, rows_per_group, out_dtype):
    t, d = x.shape
    tr = _pick(rows_per_group, (256, 128, 64, 32, 16, 8))
    tiles = rows_per_group // tr
    modulate = sc is not None
    in_specs = [pl.BlockSpec((tr, d), lambda i: (i, 0)), pl.BlockSpec((1, d), lambda i: (0, 0))]
    args = [x, g.reshape(1, d)]
    if modulate:
        in_specs += [pl.BlockSpec((None, 1, d), lambda i: (i // tiles, 0, 0))] * 2
        args += [sc, sh]
    return pl.pallas_call(
        functools.partial(_norm_kernel, modulate=modulate),
        grid=(t // tr,),
        in_specs=in_specs,
        out_specs=pl.BlockSpec((tr, d), lambda i: (i, 0)),
        out_shape=jax.ShapeDtypeStruct((t, d), out_dtype),
        compiler_params=_cparams(("parallel",)),
        name="rmsnorm_mod",
    )(*args)


def _mm_kernel(*refs, epilogue, n_side, side_blocks, nj):
    tabs = refs[:2 * n_side]
    a_ref, w_ref = refs[2 * n_side:2 * n_side + 2]
    rest = refs[2 * n_side + 2:]
    n_extra = 2 if epilogue == "residual" else 0
    extras = rest[:n_extra]
    side_in = rest[n_extra:n_extra + n_side]
    o_ref = rest[n_extra + n_side]
    side_out = rest[n_extra + n_side + 1:]

    acc = jnp.dot(a_ref[...], w_ref[...], preferred_element_type=F32)
    if epilogue == "residual":
        res_ref, gate_ref = extras
        out = res_ref[...] + gate_ref[...] * acc
    elif epilogue == "swiglu":
        half = acc.shape[1] // 2
        out = _silu(acc[:, :half]) * acc[:, half:]
    else:
        out = acc
    o_ref[...] = out.astype(o_ref.dtype)

    t = pl.program_id(0) * nj + pl.program_id(1)
    for q in range(n_side):
        nrt, nb = side_blocks[q]

        @pl.when(t < nrt * nb)
        def _(q=q, nb=nb):
            mode = tabs[2 * q + 1][t % nb]
            side_out[q][...] = jnp.where(mode == 2, 0.0, side_in[q][...]).astype(BF16)


def _mm_call(a, w, *, tm, tn, epilogue="plain", res=None, gate=None, rows_per_group=None,
             out_dtype=F32, name="matmul", side=()):
    m = a.shape[0]
    kdim, n = w.shape
    n_out = n // 2 if epilogue == "swiglu" else n
    tn_out = tn // 2 if epilogue == "swiglu" else tn
    ni, nj = m // tm, n // tn
    n_side = len(side)
    in_specs = [pl.BlockSpec((tm, kdim), lambda i, j, *_: (i, 0)),
                pl.BlockSpec((kdim, tn), lambda i, j, *_: (0, j))]
    args = [a, w]
    if epilogue == "residual":
        in_specs.append(pl.BlockSpec((tm, tn), lambda i, j, *_: (i, j)))
        if rows_per_group % tm == 0:
            tiles = rows_per_group // tm
            in_specs.append(pl.BlockSpec((None, 1, tn), lambda i, j, *_: (i // tiles, 0, j)))
        else:
            gate = jnp.repeat(gate[:, 0, :], rows_per_group, axis=0)
            in_specs.append(pl.BlockSpec((tm, tn), lambda i, j, *_: (i, j)))
        args += [res, gate]
    out_specs = [pl.BlockSpec((tm, tn_out), lambda i, j, *_: (i, j))]
    out_shape = [jax.ShapeDtypeStruct((m, n_out), out_dtype)]
    tabs, side_blocks = [], []
    for q, (src, tr, width, src_blk, mode) in enumerate(side):
        nrt, nb = src.shape[0] // tr, len(src_blk)
        assert nrt * nb <= ni * nj, "not enough grid steps for this side job"
        side_blocks.append((nrt, nb))
        tabs += [jnp.asarray(np.asarray(src_blk, np.int32)), jnp.asarray(np.asarray(mode, np.int32))]

        def src_map(i, j, *t, q=q, nrt=nrt, nb=nb):
            step = jnp.minimum(i * nj + j, nrt * nb - 1)
            return (step // nb, t[2 * q][step % nb])

        def dst_map(i, j, *t, nrt=nrt, nb=nb):
            step = jnp.minimum(i * nj + j, nrt * nb - 1)
            return (step // nb, step % nb)

        in_specs.append(pl.BlockSpec((tr, width), src_map))
        args.append(src)
        out_specs.append(pl.BlockSpec((tr, width), dst_map))
        out_shape.append(jax.ShapeDtypeStruct((src.shape[0], nb * width), BF16))
    grid_spec = pltpu.PrefetchScalarGridSpec(num_scalar_prefetch=2 * n_side, grid=(ni, nj),
                                            in_specs=in_specs, out_specs=out_specs)
    outs = pl.pallas_call(
        functools.partial(_mm_kernel, epilogue=epilogue, n_side=n_side, side_blocks=tuple(side_blocks), nj=nj),
        grid_spec=grid_spec,
        out_shape=out_shape,
        compiler_params=_cparams(("arbitrary", "arbitrary") if n_side else ("parallel", "parallel")),
        name=name,
    )(*tabs, *args)
    return outs if n_side else outs[0]


def _rope_tables(pos, head_dim):
    half = head_dim // 2
    inv = ROPE_THETA ** (-jnp.arange(half, dtype=F32) / half)
    ang = pos.astype(F32)[:, None] * inv[None, :]
    cos, sin = jnp.cos(ang), jnp.sin(ang)
    reps = LANES // head_dim
    cos_t = jnp.tile(jnp.concatenate([cos, cos], axis=1), (1, reps))
    sin_t = jnp.tile(jnp.concatenate([-sin, sin], axis=1), (1, reps))
    return cos_t, sin_t


def _rope_block(x, cos_t, sin_t, first_half):
    partner = jnp.where(first_half, pltpu.roll(x, 96, 1), pltpu.roll(x, 32, 1))
    return x * cos_t + partner * sin_t


def _first_half_mask(rows):
    lane = lax.broadcasted_iota(jnp.int32, (rows, LANES), 1)
    return (lane % 64) < 32


def _rope_k_kernel(k_ref, cos_ref, sin_ref, o_ref):
    rows, width = k_ref.shape
    fh = _first_half_mask(rows)
    cos_t, sin_t = cos_ref[...], sin_ref[...]
    for c in range(width // LANES):
        sl = slice(c * LANES, (c + 1) * LANES)
        o_ref[:, sl] = _rope_block(k_ref[:, sl].astype(F32), cos_t, sin_t, fh)


def _rope_k_call(proj, col_block, width, cos_t, sin_t, rows_per_group):
    t = proj.shape[0]
    tr = _pick(rows_per_group, (256, 128, 64, 32, 16, 8))
    tiles = rows_per_group // tr
    return pl.pallas_call(
        _rope_k_kernel,
        grid=(t // tr,),
        in_specs=[pl.BlockSpec((tr, width), lambda i: (i, col_block)),
                  pl.BlockSpec((tr, LANES), lambda i: (i % tiles, 0)),
                  pl.BlockSpec((tr, LANES), lambda i: (i % tiles, 0))],
        out_specs=pl.BlockSpec((tr, width), lambda i: (i, 0)),
        out_shape=jax.ShapeDtypeStruct((t, width), F32),
        compiler_params=_cparams(("parallel",)),
        name="rope_k",
    )(proj, cos_t, sin_t)


def _gla_kernel(q_ref, k_ref, v_ref, ga_ref, ra_ref, wgk_ref, bgk_ref, g_ref, s0_ref,
                o_ref, sout_ref, s_scr, *, n_steps, c, scale, hp):
    n = pl.program_id(2)
    rows = q_ref.shape[0]
    dk = q_ref.shape[1] // hp
    dv = v_ref.shape[1] // hp
    nr = rows // c
    nt = (((1,), (1,)), ((), ()))
    tn = (((0,), (0,)), ((), ()))
    heads = range(hp)
    ksl = lambda u: slice(u * dk, (u + 1) * dk)
    vsl = lambda u: slice(u * dv, (u + 1) * dv)

    @pl.when(n == 0)
    def _():
        s_scr[...] = s0_ref[...]

    ra = ra_ref[...]
    lane = lax.broadcasted_iota(jnp.int32, ra.shape, 1)
    ra_hi = ra.astype(BF16)
    ra_lo = (ra - ra_hi.astype(F32)).astype(BF16)
    lhs = jnp.where(jnp.logical_and(lane >= GLA_GATE_RANK, lane < 2 * GLA_GATE_RANK), ra_lo, ra_hi)
    z = jnp.dot(lhs, wgk_ref[...], preferred_element_type=F32) + bgk_ref[...]
    log_a = (jnp.minimum(z, 0.0) - jnp.log(1.0 + jnp.exp(-jnp.abs(z)))) * (1.0 / GLA_GATE_NORM)

    ri = lax.broadcasted_iota(jnp.int32, (rows, 2 * rows), 0)
    ci = lax.broadcasted_iota(jnp.int32, (rows, 2 * rows), 1) % rows
    tril2 = jnp.where(jnp.logical_and(ri // c == ci // c, ci <= ri), 1.0, 0.0).astype(BF16)
    la_hi = log_a.astype(BF16)
    la_lo = (log_a - la_hi.astype(F32)).astype(BF16)
    b_all = jnp.dot(tril2, jnp.concatenate([la_hi, la_lo], axis=0), preferred_element_type=F32)

    row = lax.broadcasted_iota(jnp.int32, (c, c), 0)
    col = lax.broadcasted_iota(jnp.int32, (c, c), 1)
    causal = row >= col

    vs, qds, kds, kd32, o_parts = [], [], [], [], []
    cum = [None]
    for r in range(nr):
        sl = slice(r * c, (r + 1) * c)
        b = b_all[sl, :]
        b_mid = b[c // 2:c // 2 + 1, :]
        b_last = b[c - 1:c, :]
        q = q_ref[sl, :].astype(F32) * scale
        k = k_ref[sl, :].astype(F32)
        v = v_ref[sl, :]
        qs = (q * jnp.exp(b - b_mid)).astype(BF16)
        ks = (k * jnp.exp(b_mid - b)).astype(BF16)
        sc = [lax.dot_general(qs[:, ksl(u)], ks[:, ksl(u)], nt, preferred_element_type=F32) for u in heads]
        o = [jnp.dot(jnp.where(causal, sc[u], 0.0).astype(BF16), v[:, vsl(u)], preferred_element_type=F32)
             for u in heads]
        qd = q * jnp.exp(b)
        for rp in range(r):
            fac = cum[r] - cum[rp + 1] if rp + 1 < r else None
            qx = (qd if fac is None else qd * jnp.exp(fac)).astype(BF16)
            sx = [lax.dot_general(qx[:, ksl(u)], kds[rp][:, ksl(u)], nt, preferred_element_type=F32)
                  for u in heads]
            o = [o[u] + jnp.dot(sx[u].astype(BF16), vs[rp][:, vsl(u)], preferred_element_type=F32)
                 for u in heads]
        qds.append(qd if cum[r] is None else qd * jnp.exp(cum[r]))
        kd = k * jnp.exp(b_last - b)
        kd32.append(kd)
        kds.append(kd.astype(BF16))
        vs.append(v)
        o_parts.append(o)
        cum.append(b_last if cum[r] is None else cum[r] + b_last)

    q_all = jnp.concatenate(qds, axis=0).astype(BF16) if nr > 1 else qds[0].astype(BF16)
    total = cum[nr]
    k_parts = []
    for r in range(nr):
        if r + 1 < nr:
            k_parts.append((kd32[r] * jnp.exp(total - cum[r + 1])).astype(BF16))
        else:
            k_parts.append(kds[r])
    k_all = jnp.concatenate(k_parts, axis=0) if nr > 1 else k_parts[0]
    dec = jnp.exp(total)
    wblk = 2 * LANES
    dec_w = []
    for u in heads:
        dec_col = jnp.transpose(jnp.broadcast_to(dec[:, ksl(u)], (LANES, dk)))
        dec_w.append(jnp.concatenate([dec_col] * (wblk // LANES), axis=1))
    o_inter = [[] for _ in heads]
    for j in range(dv // wblk):
        for u in heads:
            sl = slice(j * wblk, (j + 1) * wblk)
            vcol = slice(u * dv + j * wblk, u * dv + (j + 1) * wblk)
            s_blk = s_scr[u, :, sl]
            o_inter[u].append(jnp.dot(q_all[:, ksl(u)], s_blk.astype(BF16), preferred_element_type=F32))
            upd = lax.dot_general(k_all[:, ksl(u)], v_ref[:, vcol], tn, preferred_element_type=F32)
            s_scr[u, :, sl] = s_blk * dec_w[u] + upd

    g = g_ref[...]
    for r in range(nr):
        sl = slice(r * c, (r + 1) * c)
        for u in heads:
            o = o_parts[r][u] + jnp.concatenate([p[sl, :] for p in o_inter[u]], axis=1)
            on = o * lax.rsqrt(jnp.mean(o * o, axis=-1, keepdims=True) + NORM_EPS) * g
            o_ref[sl, vsl(u)] = (on * _silu(ga_ref[sl, vsl(u)].astype(F32))).astype(o_ref.dtype)

    @pl.when(n == n_steps - 1)
    def _():
        sout_ref[...] = s_scr[...]


def _gla_call(proj, ra3, wgk3, b_gk, g_gla_out, s0, batch, seq):
    h = GLA_HEADS
    hp = GLA_HEADS_PER_STEP
    hg = h // hp
    dk = wgk3.shape[1] // h
    dv = g_gla_out.shape[0]
    c = min(CHUNK, seq)
    rows = c * GLA_BLOCK_CHUNKS if seq % (c * GLA_BLOCK_CHUNKS) == 0 else c
    ns = seq // rows
    t = batch * seq
    wk, wv = hp * dk, hp * dv
    k_off = (h * dk) // wk
    v_off = (2 * h * dk) // wv
    g_off = v_off + hg
    row = lambda b, hh, n: b * ns + n
    kern = functools.partial(_gla_kernel, n_steps=ns, c=c, scale=dk ** -0.5, hp=hp)
    return pl.pallas_call(
        kern,
        grid=(batch, hg, ns),
        in_specs=[
            pl.BlockSpec((rows, wk), lambda b, hh, n: (row(b, hh, n), hh)),
            pl.BlockSpec((rows, wk), lambda b, hh, n: (row(b, hh, n), k_off + hh)),
            pl.BlockSpec((rows, wv), lambda b, hh, n: (row(b, hh, n), v_off + hh)),
            pl.BlockSpec((rows, wv), lambda b, hh, n: (row(b, hh, n), g_off + hh)),
            pl.BlockSpec((rows, LANES), lambda b, hh, n: (row(b, hh, n), 0)),
            pl.BlockSpec((LANES, wk), lambda b, hh, n: (0, hh)),
            pl.BlockSpec((1, wk), lambda b, hh, n: (0, hh)),
            pl.BlockSpec((1, dv), lambda b, hh, n: (0, 0)),
            pl.BlockSpec((None, hp, dk, dv), lambda b, hh, n: (b, hh, 0, 0)),
        ],
        out_specs=[
            pl.BlockSpec((rows, wv), lambda b, hh, n: (row(b, hh, n), hh)),
            pl.BlockSpec((None, hp, dk, dv), lambda b, hh, n: (b, hh, 0, 0)),
        ],
        out_shape=[jax.ShapeDtypeStruct((t, h * dv), BF16),
                   jax.ShapeDtypeStruct((batch, h, dk, dv), F32)],
        scratch_shapes=[pltpu.VMEM((hp, dk, dv), F32)],
        compiler_params=_cparams(("parallel", "parallel", "arbitrary")),
        name="gla_chunked",
    )(proj, proj, proj, proj, ra3, wgk3, b_gk.reshape(1, -1), g_gla_out.reshape(1, -1), s0)


def _dup_half(x, half, lane_lo):
    rolled = pltpu.roll(x, 64, 1)
    return jnp.where(lane_lo == (half == 0), x, rolled)


def _swa_kernel(*refs, tq, n_ka, masked):
    bias_ref, q_ref, cos_ref, sin_ref = refs[0:4]
    ka_refs = refs[4:4 + n_ka]
    va_refs = refs[4 + n_ka:4 + 2 * n_ka]
    kb_ref, vb_ref, oa_ref, za_ref, zb_ref, o_ref = refs[4 + 2 * n_ka:]

    hd = LANES // 2
    grows = SWA_GROUP * tq
    fh = _first_half_mask(tq)
    cos_t, sin_t = cos_ref[...], sin_ref[...]
    q_scale = hd ** -0.5 * LOG2E

    lane_q = lax.broadcasted_iota(jnp.int32, (tq, LANES), 1) < hd
    lane_a = lax.broadcasted_iota(jnp.int32, (WINDOW, LANES), 1) < hd
    ones = jnp.ones((2 * LANES, LANES), BF16)
    zpad = jnp.zeros((LANES - tq, LANES), BF16)

    if masked:
        n = pl.program_id(1)
        valid_a = lax.broadcasted_iota(jnp.int32, (grows, LANES), 1) >= (WINDOW - n * tq)

    nt = (((1,), (1,)), ((), ()))

    def keys_vals(h):
        pc, half = h // 2, h % 2
        sl = slice(pc * LANES, (pc + 1) * LANES)
        ka = jnp.concatenate([r[:, sl] for r in ka_refs], axis=0) if n_ka > 1 else ka_refs[0][:, sl]
        va = jnp.concatenate([r[:, sl] for r in va_refs], axis=0) if n_ka > 1 else va_refs[0][:, sl]
        k_all = jnp.concatenate([_dup_half(ka.astype(F32), half, lane_a).astype(BF16),
                                 _dup_half(kb_ref[:, sl].astype(F32), half, lane_q).astype(BF16), zpad], axis=0)
        v_all = jnp.concatenate([_dup_half(va.astype(F32), half, lane_a).astype(BF16),
                                 _dup_half(vb_ref[:, sl].astype(F32), half, lane_q).astype(BF16), zpad], axis=0)
        return k_all, v_all

    def queries(h):
        rows = []
        for g in range(SWA_GROUP):
            cb = h * (SWA_GROUP // 2) + g // 2
            qsl = slice(cb * LANES, (cb + 1) * LANES)
            qr = _rope_block(q_ref[:, qsl].astype(F32), cos_t, sin_t, fh) * q_scale
            rows.append(jnp.where(lane_q == (g % 2 == 0), qr, 0.0).astype(BF16))
        return jnp.concatenate(rows, axis=0)

    def finish(h, out2):
        for j in range(SWA_GROUP // 2):
            cb = h * (SWA_GROUP // 2) + j
            osl = slice(cb * LANES, (cb + 1) * LANES)
            ob = jnp.where(lane_q, out2[(2 * j) * tq:(2 * j + 1) * tq, :],
                           out2[(2 * j + 1) * tq:(2 * j + 2) * tq, :])
            merged = (_sigmoid(za_ref[:, osl].astype(F32)) * oa_ref[:, osl].astype(F32)
                      + _sigmoid(zb_ref[:, osl].astype(F32)) * ob)
            o_ref[:, osl] = merged.astype(o_ref.dtype)

    hs_per = SWA_HEADS_PER_STAGE
    for h0 in range(0, SWA_KV_HEADS, hs_per):
        hs = range(h0, h0 + hs_per)
        kv = [keys_vals(h) for h in hs]
        lhs = [queries(h) for h in hs]
        logits = [lax.dot_general(lhs[i], kv[i][0], nt, preferred_element_type=F32) for i in range(hs_per)]
        l1 = [jnp.where(valid_a, lg[:, :LANES], -jnp.inf) if masked else lg[:, :LANES] for lg in logits]
        l2 = [logits[i][:, LANES:] + bias_ref[h0 + i] for i in range(hs_per)]
        m = [jnp.max(jnp.maximum(l1[i], l2[i]), axis=-1, keepdims=True) for i in range(hs_per)]
        p = [jnp.concatenate([jnp.exp2(l1[i] - m[i]), jnp.exp2(l2[i] - m[i])], axis=1).astype(BF16)
             for i in range(hs_per)]
        pv = [jnp.dot(p[i], kv[i][1], preferred_element_type=F32) for i in range(hs_per)]
        den = [jnp.dot(p[i], ones, preferred_element_type=F32) for i in range(hs_per)]
        for i in range(hs_per):
            finish(h0 + i, pv[i] / den[i])


def _swa_call(proj, q_blk, k_rot, o_a, sinks, cos_t, sin_t, batch, seq, cache_k=None, cache_v=None):
    t, d = o_a.shape
    kvw = k_rot.shape[1]
    v_blk = ((q_blk + 3) * d) // kvw + 1
    if cache_k is None:
        tq = CHUNK
        nq = seq // tq
        rowi = lambda b, n: b * nq + n
        prev = lambda b, n, back: b * nq + jnp.maximum(n - back, 0)
        n_ka = WINDOW // CHUNK
        ka_args = [k_rot] * n_ka
        va_args = [proj] * n_ka
        ka_specs = [pl.BlockSpec((tq, kvw), functools.partial(lambda b, n, back: (prev(b, n, back), 0),
                                                               back=n_ka - i)) for i in range(n_ka)]
        va_specs = [pl.BlockSpec((tq, kvw), functools.partial(lambda b, n, back: (prev(b, n, back), v_blk),
                                                               back=n_ka - i)) for i in range(n_ka)]
        masked = True
    else:
        tq = seq
        nq = 1
        rowi = lambda b, n: b
        n_ka = 1
        ka_args, va_args = [cache_k], [cache_v]
        ka_specs = [pl.BlockSpec((None, WINDOW, kvw), lambda b, n: (b, 0, 0))]
        va_specs = [pl.BlockSpec((None, WINDOW, kvw), lambda b, n: (b, 0, 0))]
        masked = False
    wide = lambda blk: pl.BlockSpec((tq, d), lambda b, n: (rowi(b, n), blk))
    tab = pl.BlockSpec((tq, LANES), lambda b, n: (n, 0))
    lane = jnp.arange(LANES)
    sink_l2 = (sinks.astype(F32) * LOG2E).reshape(SWA_KV_HEADS, SWA_GROUP, 1, 1)
    bias_row = jnp.where(lane < tq, 0.0, jnp.where(lane == tq, sink_l2, -jnp.inf))
    sinks = jnp.broadcast_to(bias_row, (SWA_KV_HEADS, SWA_GROUP, tq, LANES)).reshape(
        SWA_KV_HEADS, SWA_GROUP * tq, LANES)
    in_specs = ([pl.BlockSpec(sinks.shape, lambda b, n: (0, 0, 0)), wide(q_blk), tab, tab] + ka_specs + va_specs +
                [pl.BlockSpec((tq, kvw), lambda b, n: (rowi(b, n), 0)),
                 pl.BlockSpec((tq, kvw), lambda b, n: (rowi(b, n), v_blk)),
                 wide(0), wide(q_blk + 1), wide(q_blk + 2)])
    args = [sinks, proj, cos_t, sin_t] + ka_args + va_args + [k_rot, proj, o_a, proj, proj]
    return pl.pallas_call(
        functools.partial(_swa_kernel, tq=tq, n_ka=n_ka, masked=masked),
        grid=(batch, nq),
        in_specs=in_specs,
        out_specs=pl.BlockSpec((tq, d), lambda b, n: (rowi(b, n), 0)),
        out_shape=jax.ShapeDtypeStruct((t, d), BF16),
        compiler_params=_cparams(("parallel", "arbitrary")),
        name="swa_merge",
    )(*args)


def _prep_weights(w_in, w_out, w_gate_up, w_down, w_gk_up, d):
    h = GLA_HEADS
    key = (d // (2 * h)) * h
    n1 = 2 * key + 2 * d
    kvw = SWA_KV_HEADS * (d // SWA_HEADS)
    w = kvw
    nb1 = n1 // w
    per_d = d // w
    q_s = list(range(per_d))
    k_s, v_s = per_d, per_d + 1
    za_s = [per_d + 2 + i for i in range(per_d)]
    zb_s = [2 * per_d + 2 + i for i in range(per_d)]
    shifted = q_s + za_s + zb_s + [k_s, v_s]
    src_blk = list(range(nb1)) + [nb1 + s for s in shifted]
    mode = [0] * nb1 + [1] * len(shifted)
    w_in_t = jnp.transpose(w_in)
    w12 = _relayout_t_call(w_in_t, w, src_blk, mode, shift=GLA_GATE_RANK, tk=1024)

    ra_cols = jnp.transpose(w_in_t[n1:n1 + GLA_GATE_RANK, :])
    w_r = jnp.pad(jnp.concatenate([ra_cols] * 3, axis=1),
                  ((0, 0), (0, LANES - 3 * GLA_GATE_RANK))).astype(BF16)
    wg_hi = w_gk_up.astype(BF16)
    wg_lo = (w_gk_up - wg_hi.astype(F32)).astype(BF16)
    wgk3 = jnp.pad(jnp.concatenate([wg_hi, wg_hi, wg_lo], axis=0),
                   ((0, LANES - 3 * GLA_GATE_RANK), (0, 0)))

    lanes_blk = list(range(w_out.shape[1] // LANES))
    job_o = (w_out, w_out.shape[0], LANES, lanes_blk, [0] * len(lanes_blk))

    dff = w_down.shape[0]
    n_src = dff // LANES
    n_tiles = -(-dff // FF_TILE)
    per_tile = FF_TILE // LANES
    src_blk, mode = [], []
    for j in range(n_tiles):
        for part in range(2):
            for sb in range(per_tile):
                blk = j * per_tile + sb
                ok = blk < n_src
                src_blk.append(part * n_src + blk if ok else 0)
                mode.append(0 if ok else 2)
    job_gu = (w_gate_up, w_gate_up.shape[0], LANES, src_blk, mode)
    cols_dn = list(range(d // 256))
    job_dn = (w_down, dff // 4, 256, cols_dn, [0] * len(cols_dn))
    return w12, w_r, wgk3, job_o, job_gu, job_dn


def _run_job(job):
    src, tr, width, src_blk, mode = job
    return _relayout_call(src, width, src_blk, mode, tr=tr)


def _layer(x, mod, pos, s0, cache_k, cache_v, weights, params, batch, seq):
    w12, w_r, wgk3, w_o, w_gu, w_dn = weights
    g_norm1, b_gk, g_gla_out, sinks, g_norm2 = params
    t, d = x.shape
    sh1, sc1, gt1, sh2, sc2, gt2 = [m.reshape(batch, 1, d) for m in jnp.split(mod, 6, axis=-1)]
    kvw = SWA_KV_HEADS * (d // SWA_HEADS)
    hd = d // SWA_HEADS
    tm = min(MM_TILE_M, t)
    q_blk = (w12.shape[1] - 3 * d - 2 * kvw) // d

    def n_blocks(job):
        return (job[0].shape[0] // job[1]) * len(job[3])

    h = _norm_call(x, g_norm1, sc1, sh1, seq, BF16)
    steps = (t // tm) * (w12.shape[1] // 1024)
    if isinstance(w_gu, tuple) and max(n_blocks(w_gu), n_blocks(w_o)) <= steps:
        proj, w_gu, w_o = _mm_call(h, w12, tm=tm, tn=1024, out_dtype=BF16, name="in_proj", side=(w_gu, w_o))
    else:
        proj = _mm_call(h, w12, tm=tm, tn=1024, out_dtype=BF16, name="in_proj")
        if isinstance(w_gu, tuple):
            w_gu, w_o = _run_job(w_gu), _run_job(w_o)
    ra3 = _mm_call(h, w_r, tm=tm, tn=LANES, name="in_proj_rank")

    o_a, s_new = _gla_call(proj, ra3, wgk3, b_gk, g_gla_out, s0, batch, seq)

    cos_t, sin_t = _rope_tables(pos, hd)
    k_blk = ((q_blk + 3) * d) // kvw
    k_rot = _rope_k_call(proj, k_blk, kvw, cos_t, sin_t, seq)
    merged = _swa_call(proj, q_blk, k_rot, o_a, sinks, cos_t, sin_t, batch, seq, cache_k, cache_v)

    x1 = _mm_call(merged, w_o, tm=tm, tn=1024, epilogue="residual", res=x, gate=gt1,
                  rows_per_group=seq, name="out_proj")
    h2 = _norm_call(x1, g_norm2, sc2, sh2, seq, BF16)
    steps = (t // tm) * (w_gu.shape[1] // (2 * FF_TILE))
    if isinstance(w_dn, tuple) and n_blocks(w_dn) <= steps:
        act, w_dn = _mm_call(h2, w_gu, tm=tm, tn=2 * FF_TILE, epilogue="swiglu", out_dtype=BF16,
                             name="ffn_gate_up", side=(w_dn,))
    else:
        act = _mm_call(h2, w_gu, tm=tm, tn=2 * FF_TILE, epilogue="swiglu", out_dtype=BF16, name="ffn_gate_up")
        if isinstance(w_dn, tuple):
            w_dn = _run_job(w_dn)
    x2 = _mm_call(act, w_dn, tm=min(MM_TILE_M // 2, t), tn=512, epilogue="residual", res=x1, gate=gt2,
                  rows_per_group=seq, name="ffn_down")

    keep = WINDOW if cache_k is None else seq
    k_keep = jnp.stack([k_rot[(b + 1) * seq - keep:(b + 1) * seq] for b in range(batch)]).reshape(
        batch, keep, SWA_KV_HEADS, hd)
    v_cols = slice((k_blk + 1) * kvw, (k_blk + 2) * kvw)
    v_new = jnp.stack([proj[(b + 1) * seq - keep:(b + 1) * seq, v_cols] for b in range(batch)])
    v_keep = v_new.astype(F32).reshape(batch, keep, SWA_KV_HEADS, hd)
    return x2, s_new, k_keep, v_keep, (w12, w_r, wgk3, w_o, w_gu, w_dn)


def kernel(x_prompt, x_sample, state_gla, cache_swa_k, cache_swa_v, c_prompt, c_sample, w_ada, b_ada,
           g_norm1, w_in, w_gk_up, b_gk, g_gla_out, swa_sinks, w_out, g_norm2, w_gate_up, w_down, g_final):
    bp, sp, d = x_prompt.shape
    bs, ss, _ = x_sample.shape
    depth = w_ada.shape[0]
    dk = w_gk_up.shape[2] // GLA_HEADS
    dv = g_gla_out.shape[1]
    kvw = SWA_KV_HEADS * (d // SWA_HEADS)

    pos_p = jnp.arange(sp)
    pos_s = PAST_LEN + jnp.arange(ss)
    s0_p = jnp.zeros((bp, GLA_HEADS, dk, dv), F32)

    nc = bp + bs
    nc_pad = -(-nc // 16) * 16
    c_all = jnp.concatenate([c_prompt, c_sample, jnp.zeros((nc_pad - nc, d), F32)], axis=0)

    hp = x_prompt.reshape(bp * sp, d)
    hs = x_sample.reshape(bs * ss, d)
    outs = [[] for _ in range(6)]
    for l in range(depth):
        mod = _mod_call(c_all, w_ada[l], b_ada[l])
        weights = _prep_weights(w_in[l], w_out[l], w_gate_up[l], w_down[l], w_gk_up[l], d)
        params = (g_norm1[l], b_gk[l], g_gla_out[l], swa_sinks[l], g_norm2[l])
        hp, s_p, k_p, v_p, weights = _layer(hp, mod[:bp], pos_p, s0_p, None, None, weights, params, bp, sp)
        hs, s_s, k_s, v_s, _ = _layer(hs, mod[bp:nc], pos_s, state_gla[l],
                                   cache_swa_k[l].reshape(bs, WINDOW, kvw),
                                   cache_swa_v[l].reshape(bs, WINDOW, kvw), weights, params, bs, ss)
        for lst, val in zip(outs, (s_p, k_p, v_p, s_s, k_s, v_s)):
            lst.append(val)
    y_prompt = _norm_call(hp, g_final, None, None, sp, F32).reshape(bp, sp, d)
    y_sample = _norm_call(hs, g_final, None, None, ss, F32).reshape(bs, ss, d)
    return (y_prompt, y_sample) + tuple(jnp.stack(o) for o in outs)
```

```python
import functools

import jax
import jax.numpy as jnp
import numpy as np
from jax import lax
from jax.experimental import pallas as pl
from jax.experimental.pallas import tpu as pltpu

F32 = jnp.float32
BF16 = jnp.bfloat16

CHUNK = 64
GLA_HEADS = 4
GLA_GATE_RANK = 16
GLA_GATE_NORM = 16.0
SWA_HEADS = 64
SWA_KV_HEADS = 8
SWA_GROUP = SWA_HEADS // SWA_KV_HEADS
WINDOW = 128
ROPE_THETA = 10000.0
NORM_EPS = 1e-6
PAST_LEN = 2048
LOG2E = 1.4426950408889634

LANES = 128
VMEM_LIMIT_BYTES = 56 * 1024 * 1024
MM_TILE_M = 1024
GLA_HEADS_PER_STEP = 4
GLA_BLOCK_CHUNKS = 4
SWA_BLOCKS_PER_STEP = 2
SWA_HEADS_PER_STAGE = 4
FF_TILE = 512


def _cparams(sem):
    return pltpu.CompilerParams(dimension_semantics=sem, vmem_limit_bytes=VMEM_LIMIT_BYTES)


def _silu(x):
    return x * (1.0 / (1.0 + jnp.exp(-x)))


def _sigmoid(x):
    return 1.0 / (1.0 + jnp.exp(-x))


def _pick(n, prefs):
    for p in prefs:
        if n % p == 0:
            return p
    return n


def _relayout_kernel(src_ref, mode_ref, main, o_ref):
    mode = mode_ref[pl.program_id(1)]

    @pl.when(mode == 0)
    def _():
        o_ref[...] = main[...].astype(BF16)

    @pl.when(mode == 2)
    def _():
        o_ref[...] = jnp.zeros(o_ref.shape, BF16)


def _relayout_call(src, width, src_blk, mode, tr=None):
    r = src.shape[0]
    tr = r if tr is None else tr
    nb = len(src_blk)
    grid_spec = pltpu.PrefetchScalarGridSpec(
        num_scalar_prefetch=2,
        grid=(r // tr, nb),
        in_specs=[pl.BlockSpec((tr, width), lambda i, b, s, m: (i, s[b]))],
        out_specs=pl.BlockSpec((tr, width), lambda i, b, s, m: (i, b)),
    )
    return pl.pallas_call(
        _relayout_kernel,
        grid_spec=grid_spec,
        out_shape=jax.ShapeDtypeStruct((r, nb * width), BF16),
        compiler_params=_cparams(("parallel", "arbitrary")),
        name="weight_relayout",
    )(jnp.asarray(np.asarray(src_blk, np.int32)), jnp.asarray(np.asarray(mode, np.int32)), src)


def _relayout_t_kernel(src_ref, nxt_ref, mode_ref, main, nxt, o_ref, *, shift):
    mode = mode_ref[pl.program_id(1)]
    w = o_ref.shape[1]

    @pl.when(mode == 0)
    def _():
        o_ref[...] = jnp.transpose(main[...]).astype(BF16)

    @pl.when(mode == 1)
    def _():
        x = jnp.concatenate([main[...], nxt[...]], axis=0)
        o_ref[...] = jnp.transpose(x[shift:shift + w, :]).astype(BF16)


def _relayout_t_call(src_t, width, src_blk, mode, shift, tk):
    r = src_t.shape[1]
    nb = len(src_blk)
    src_blk = np.asarray(src_blk, np.int32)
    nxt_blk = (src_blk + 1) * (width // shift)
    grid_spec = pltpu.PrefetchScalarGridSpec(
        num_scalar_prefetch=3,
        grid=(r // tk, nb),
        in_specs=[pl.BlockSpec((width, tk), lambda i, b, s, nx, m: (s[b], i)),
                  pl.BlockSpec((shift, tk), lambda i, b, s, nx, m: (nx[b], i))],
        out_specs=pl.BlockSpec((tk, width), lambda i, b, s, nx, m: (i, b)),
    )
    return pl.pallas_call(
        functools.partial(_relayout_t_kernel, shift=shift),
        grid_spec=grid_spec,
        out_shape=jax.ShapeDtypeStruct((r, nb * width), BF16),
        compiler_params=_cparams(("parallel", "arbitrary")),
        name="weight_relayout_t",
    )(jnp.asarray(src_blk), jnp.asarray(nxt_blk, jnp.int32), jnp.asarray(np.asarray(mode, np.int32)),
      src_t, src_t)


def _mod_kernel(c_ref, w_ref, b_ref, o_ref):
    a = _silu(c_ref[...]).astype(BF16)
    w = w_ref[...].astype(BF16)
    o_ref[...] = jnp.dot(a, w, preferred_element_type=F32) + b_ref[...]


def _mod_call(c_all, w_ada, b_ada):
    m, d = c_all.shape
    n = w_ada.shape[1]
    tn = _pick(n, (512, 256, 128))
    return pl.pallas_call(
        _mod_kernel,
        grid=(n // tn,),
        in_specs=[pl.BlockSpec((m, d), lambda j: (0, 0)),
                  pl.BlockSpec((d, tn), lambda j: (0, j)),
                  pl.BlockSpec((1, tn), lambda j: (0, j))],
        out_specs=pl.BlockSpec((m, tn), lambda j: (0, j)),
        out_shape=jax.ShapeDtypeStruct((m, n), F32),
        compiler_params=_cparams(("arbitrary",)),
        name="adaln_mod",
    )(c_all, w_ada, b_ada.reshape(1, n))


def _norm_kernel(x_ref, g_ref, *rest, modulate):
    o_ref = rest[-1]
    x = x_ref[...]
    y = x * lax.rsqrt(jnp.mean(x * x, axis=-1, keepdims=True) + NORM_EPS) * g_ref[...]
    if modulate:
        sc_ref, sh_ref = rest[0], rest[1]
        y = y * (1.0 + sc_ref[...]) + sh_ref[...]
    o_ref[...] = y.astype(o_ref.dtype)


def _norm_call(x, g, sc, sh, rows_per_group, out_dtype):
    t, d = x.shape
    tr = _pick(rows_per_group, (256, 128, 64, 32, 16, 8))
    tiles = rows_per_group // tr
    modulate = sc is not None
    in_specs = [pl.BlockSpec((tr, d), lambda i: (i, 0)), pl.BlockSpec((1, d), lambda i: (0, 0))]
    args = [x, g.reshape(1, d)]
    if modulate:
        in_specs += [pl.BlockSpec((None, 1, d), lambda i: (i // tiles, 0, 0))] * 2
        args += [sc, sh]
    return pl.pallas_call(
        functools.partial(_norm_kernel, modulate=modulate),
        grid=(t // tr,),
        in_specs=in_specs,
        out_specs=pl.BlockSpec((tr, d), lambda i: (i, 0)),
        out_shape=jax.ShapeDtypeStruct((t, d), out_dtype),
        compiler_params=_cparams(("parallel",)),
        name="rmsnorm_mod",
    )(*args)


def _mm_kernel(*refs, epilogue, n_side, side_blocks, nj):
    tabs = refs[:2 * n_side]
    a_ref, w_ref = refs[2 * n_side:2 * n_side + 2]
    rest = refs[2 * n_side + 2:]
    n_extra = 2 if epilogue == "residual" else 0
    extras = rest[:n_extra]
    side_in = rest[n_extra:n_extra + n_side]
    o_ref = rest[n_extra + n_side]
    side_out = rest[n_extra + n_side + 1:]

    acc = jnp.dot(a_ref[...], w_ref[...], preferred_element_type=F32)
    if epilogue == "residual":
        res_ref, gate_ref = extras
        out = res_ref[...] + gate_ref[...] * acc
    elif epilogue == "swiglu":
        half = acc.shape[1] // 2
        out = _silu(acc[:, :half]) * acc[:, half:]
    else:
        out = acc
    o_ref[...] = out.astype(o_ref.dtype)

    t = pl.program_id(0) * nj + pl.program_id(1)
    for q in range(n_side):
        nrt, nb = side_blocks[q]

        @pl.when(t < nrt * nb)
        def _(q=q, nb=nb):
            mode = tabs[2 * q + 1][t % nb]
            side_out[q][...] = jnp.where(mode == 2, 0.0, side_in[q][...]).astype(BF16)


def _mm_call(a, w, *, tm, tn, epilogue="plain", res=None, gate=None, rows_per_group=None,
             out_dtype=F32, name="matmul", side=()):
    m = a.shape[0]
    kdim, n = w.shape
    n_out = n // 2 if epilogue == "swiglu" else n
    tn_out = tn // 2 if epilogue == "swiglu" else tn
    ni, nj = m // tm, n // tn
    n_side = len(side)
    in_specs = [pl.BlockSpec((tm, kdim), lambda i, j, *_: (i, 0)),
                pl.BlockSpec((kdim, tn), lambda i, j, *_: (0, j))]
    args = [a, w]
    if epilogue == "residual":
        in_specs.append(pl.BlockSpec((tm, tn), lambda i, j, *_: (i, j)))
        if rows_per_group % tm == 0:
            tiles = rows_per_group // tm
            in_specs.append(pl.BlockSpec((None, 1, tn), lambda i, j, *_: (i // tiles, 0, j)))
        else:
            gate = jnp.repeat(gate[:, 0, :], rows_per_group, axis=0)
            in_specs.append(pl.BlockSpec((tm, tn), lambda i, j, *_: (i, j)))
        args += [res, gate]
    out_specs = [pl.BlockSpec((tm, tn_out), lambda i, j, *_: (i, j))]
    out_shape = [jax.ShapeDtypeStruct((m, n_out), out_dtype)]
    tabs, side_blocks = [], []
    for q, (src, tr, width, src_blk, mode) in enumerate(side):
        nrt, nb = src.shape[0] // tr, len(src_blk)
        assert nrt * nb <= ni * nj, "not enough grid steps for this side job"
        side_blocks.append((nrt, nb))
        tabs += [jnp.asarray(np.asarray(src_blk, np.int32)), jnp.asarray(np.asarray(mode, np.int32))]

        def src_map(i, j, *t, q=q, nrt=nrt, nb=nb):
            step = jnp.minimum(i * nj + j, nrt * nb - 1)
            return (step // nb, t[2 * q][step % nb])

        def dst_map(i, j, *t, nrt=nrt, nb=nb):
            step = jnp.minimum(i * nj + j, nrt * nb - 1)
            return (step // nb, step % nb)

        in_specs.append(pl.BlockSpec((tr, width), src_map))
        args.append(src)
        out_specs.append(pl.BlockSpec((tr, width), dst_map))
        out_shape.append(jax.ShapeDtypeStruct((src.shape[0], nb * width), BF16))
    grid_spec = pltpu.PrefetchScalarGridSpec(num_scalar_prefetch=2 * n_side, grid=(ni, nj),
                                            in_specs=in_specs, out_specs=out_specs)
    outs = pl.pallas_call(
        functools.partial(_mm_kernel, epilogue=epilogue, n_side=n_side, side_blocks=tuple(side_blocks), nj=nj),
        grid_spec=grid_spec,
        out_shape=out_shape,
        compiler_params=_cparams(("arbitrary", "arbitrary") if n_side else ("parallel", "parallel")),
        name=name,
    )(*tabs, *args)
    return outs if n_side else outs[0]


def _rope_tables(pos, head_dim):
    half = head_dim // 2
    inv = ROPE_THETA ** (-jnp.arange(half, dtype=F32) / half)
    ang = pos.astype(F32)[:, None] * inv[None, :]
    cos, sin = jnp.cos(ang), jnp.sin(ang)
    reps = LANES // head_dim
    cos_t = jnp.tile(jnp.concatenate([cos, cos], axis=1), (1, reps))
    sin_t = jnp.tile(jnp.concatenate([-sin, sin], axis=1), (1, reps))
    return cos_t, sin_t


def _rope_block(x, cos_t, sin_t, first_half):
    partner = jnp.where(first_half, pltpu.roll(x, 96, 1), pltpu.roll(x, 32, 1))
    return x * cos_t + partner * sin_t


def _first_half_mask(rows):
    lane = lax.broadcasted_iota(jnp.int32, (rows, LANES), 1)
    return (lane % 64) < 32


def _rope_k_kernel(k_ref, cos_ref, sin_ref, o_ref):
    rows, width = k_ref.shape
    fh = _first_half_mask(rows)
    cos_t, sin_t = cos_ref[...], sin_ref[...]
    for c in range(width // LANES):
        sl = slice(c * LANES, (c + 1) * LANES)
        o_ref[:, sl] = _rope_block(k_ref[:, sl].astype(F32), cos_t, sin_t, fh)


def _rope_k_call(proj, col_block, width, cos_t, sin_t, rows_per_group):
    t = proj.shape[0]
    tr = _pick(rows_per_group, (256, 128, 64, 32, 16, 8))
    tiles = rows_per_group // tr
    return pl.pallas_call(
        _rope_k_kernel,
        grid=(t // tr,),
        in_specs=[pl.BlockSpec((tr, width), lambda i: (i, col_block)),
                  pl.BlockSpec((tr, LANES), lambda i: (i % tiles, 0)),
                  pl.BlockSpec((tr, LANES), lambda i: (i % tiles, 0))],
        out_specs=pl.BlockSpec((tr, width), lambda i: (i, 0)),
        out_shape=jax.ShapeDtypeStruct((t, width), F32),
        compiler_params=_cparams(("parallel",)),
        name="rope_k",
    )(proj, cos_t, sin_t)


def _gla_kernel(q_ref, k_ref, v_ref, ga_ref, ra_ref, wgk_ref, bgk_ref, g_ref, s0_ref,
                o_ref, sout_ref, s_scr, *, n_steps, c, scale, hp):
    n = pl.program_id(2)
    rows = q_ref.shape[0]
    dk = q_ref.shape[1] // hp
    dv = v_ref.shape[1] // hp
    nr = rows // c
    nt = (((1,), (1,)), ((), ()))
    tn = (((0,), (0,)), ((), ()))
    heads = range(hp)
    ksl = lambda u: slice(u * dk, (u + 1) * dk)
    vsl = lambda u: slice(u * dv, (u + 1) * dv)

    @pl.when(n == 0)
    def _():
        s_scr[...] = s0_ref[...]

    ra = ra_ref[...]
    lane = lax.broadcasted_iota(jnp.int32, ra.shape, 1)
    ra_hi = ra.astype(BF16)
    ra_lo = (ra - ra_hi.astype(F32)).astype(BF16)
    lhs = jnp.where(jnp.logical_and(lane >= GLA_GATE_RANK, lane < 2 * GLA_GATE_RANK), ra_lo, ra_hi)
    z = jnp.dot(lhs, wgk_ref[...], preferred_element_type=F32) + bgk_ref[...]
    log_a = (jnp.minimum(z, 0.0) - jnp.log(1.0 + jnp.exp(-jnp.abs(z)))) * (1.0 / GLA_GATE_NORM)

    ri = lax.broadcasted_iota(jnp.int32, (rows, 2 * rows), 0)
    ci = lax.broadcasted_iota(jnp.int32, (rows, 2 * rows), 1) % rows
    tril2 = jnp.where(jnp.logical_and(ri // c == ci // c, ci <= ri), 1.0, 0.0).astype(BF16)
    la_hi = log_a.astype(BF16)
    la_lo = (log_a - la_hi.astype(F32)).astype(BF16)
    b_all = jnp.dot(tril2, jnp.concatenate([la_hi, la_lo], axis=0), preferred_element_type=F32)

    row = lax.broadcasted_iota(jnp.int32, (c, c), 0)
    col = lax.broadcasted_iota(jnp.int32, (c, c), 1)
    causal = row >= col

    vs, qds, kds, kd32, o_parts = [], [], [], [], []
    cum = [None]
    for r in range(nr):
        sl = slice(r * c, (r + 1) * c)
        b = b_all[sl, :]
        b_mid = b[c // 2:c // 2 + 1, :]
        b_last = b[c - 1:c, :]
        q = q_ref[sl, :].astype(F32) * scale
        k = k_ref[sl, :].astype(F32)
        v = v_ref[sl, :]
        qs = (q * jnp.exp(b - b_mid)).astype(BF16)
        ks = (k * jnp.exp(b_mid - b)).astype(BF16)
        sc = [lax.dot_general(qs[:, ksl(u)], ks[:, ksl(u)], nt, preferred_element_type=F32) for u in heads]
        o = [jnp.dot(jnp.where(causal, sc[u], 0.0).astype(BF16), v[:, vsl(u)], preferred_element_type=F32)
             for u in heads]
        qd = q * jnp.exp(b)
        for rp in range(r):
            fac = cum[r] - cum[rp + 1] if rp + 1 < r else None
            qx = (qd if fac is None else qd * jnp.exp(fac)).astype(BF16)
            sx = [lax.dot_general(qx[:, ksl(u)], kds[rp][:, ksl(u)], nt, preferred_element_type=F32)
                  for u in heads]
            o = [o[u] + jnp.dot(sx[u].astype(BF16), vs[rp][:, vsl(u)], preferred_element_type=F32)
                 for u in heads]
        qds.append(qd if cum[r] is None else qd * jnp.exp(cum[r]))
        kd = k * jnp.exp(b_last - b)
        kd32.append(kd)
        kds.append(kd.astype(BF16))
        vs.append(v)
        o_parts.append(o)
        cum.append(b_last if cum[r] is None else cum[r] + b_last)

    q_all = jnp.concatenate(qds, axis=0).astype(BF16) if nr > 1 else qds[0].astype(BF16)
    total = cum[nr]
    k_parts = []
    for r in range(nr):
        if r + 1 < nr:
            k_parts.append((kd32[r] * jnp.exp(total - cum[r + 1])).astype(BF16))
        else:
            k_parts.append(kds[r])
    k_all = jnp.concatenate(k_parts, axis=0) if nr > 1 else k_parts[0]
    dec = jnp.exp(total)
    wblk = 2 * LANES
    dec_w = []
    for u in heads:
        dec_col = jnp.transpose(jnp.broadcast_to(dec[:, ksl(u)], (LANES, dk)))
        dec_w.append(jnp.concatenate([dec_col] * (wblk // LANES), axis=1))
    o_inter = [[] for _ in heads]
    for j in range(dv // wblk):
        for u in heads:
            sl = slice(j * wblk, (j + 1) * wblk)
            vcol = slice(u * dv + j * wblk, u * dv + (j + 1) * wblk)
            s_blk = s_scr[u, :, sl]
            o_inter[u].append(jnp.dot(q_all[:, ksl(u)], s_blk.astype(BF16), preferred_element_type=F32))
            upd = lax.dot_general(k_all[:, ksl(u)], v_ref[:, vcol], tn, preferred_element_type=F32)
            s_scr[u, :, sl] = s_blk * dec_w[u] + upd

    g = g_ref[...]
    for r in range(nr):
        sl = slice(r * c, (r + 1) * c)
        for u in heads:
            o = o_parts[r][u] + jnp.concatenate([p[sl, :] for p in o_inter[u]], axis=1)
            on = o * lax.rsqrt(jnp.mean(o * o, axis=-1, keepdims=True) + NORM_EPS) * g
            o_ref[sl, vsl(u)] = (on * _silu(ga_ref[sl, vsl(u)].astype(F32))).astype(o_ref.dtype)

    @pl.when(n == n_steps - 1)
    def _():
        sout_ref[...] = s_scr[...]


def _gla_call(proj, ra3, wgk3, b_gk, g_gla_out, s0, batch, seq):
    h = GLA_HEADS
    hp = GLA_HEADS_PER_STEP
    hg = h // hp
    dk = wgk3.shape[1] // h
    dv = g_gla_out.shape[0]
    c = min(CHUNK, seq)
    rows = c * GLA_BLOCK_CHUNKS if seq % (c * GLA_BLOCK_CHUNKS) == 0 else c
    ns = seq // rows
    t = batch * seq
    wk, wv = hp * dk, hp * dv
    k_off = (h * dk) // wk
    v_off = (2 * h * dk) // wv
    g_off = v_off + hg
    row = lambda b, hh, n: b * ns + n
    state_mode = dict(pipeline_mode=pl.Buffered(1)) if ns > 1 else {}
    kern = functools.partial(_gla_kernel, n_steps=ns, c=c, scale=dk ** -0.5, hp=hp)
    return pl.pallas_call(
        kern,
        grid=(batch, hg, ns),
        in_specs=[
            pl.BlockSpec((rows, wk), lambda b, hh, n: (row(b, hh, n), hh)),
            pl.BlockSpec((rows, wk), lambda b, hh, n: (row(b, hh, n), k_off + hh)),
            pl.BlockSpec((rows, wv), lambda b, hh, n: (row(b, hh, n), v_off + hh)),
            pl.BlockSpec((rows, wv), lambda b, hh, n: (row(b, hh, n), g_off + hh)),
            pl.BlockSpec((rows, LANES), lambda b, hh, n: (row(b, hh, n), 0)),
            pl.BlockSpec((LANES, wk), lambda b, hh, n: (0, hh)),
            pl.BlockSpec((1, wk), lambda b, hh, n: (0, hh)),
            pl.BlockSpec((1, dv), lambda b, hh, n: (0, 0)),
            pl.BlockSpec((None, hp, dk, dv), lambda b, hh, n: (b, hh, 0, 0), **state_mode),
        ],
        out_specs=[
            pl.BlockSpec((rows, wv), lambda b, hh, n: (row(b, hh, n), hh)),
            pl.BlockSpec((None, hp, dk, dv), lambda b, hh, n: (b, hh, 0, 0), **state_mode),
        ],
        out_shape=[jax.ShapeDtypeStruct((t, h * dv), BF16),
                   jax.ShapeDtypeStruct((batch, h, dk, dv), F32)],
        scratch_shapes=[pltpu.VMEM((hp, dk, dv), F32)],
        compiler_params=_cparams(("parallel", "parallel", "arbitrary")),
        name="gla_chunked",
    )(proj, proj, proj, proj, ra3, wgk3, b_gk.reshape(1, -1), g_gla_out.reshape(1, -1), s0)


def _dup_half(x, half, lane_lo):
    rolled = pltpu.roll(x, 64, 1)
    return jnp.where(lane_lo == (half == 0), x, rolled)


def _swa_kernel(*refs, tq, n_prev, nsub, masked):
    bias_ref, q_ref, cos_ref, sin_ref = refs[0:4]
    ka_refs = refs[4:4 + n_prev]
    va_refs = refs[4 + n_prev:4 + 2 * n_prev]
    kb_ref, vb_ref, oa_ref, za_ref, zb_ref, o_ref = refs[4 + 2 * n_prev:]

    hd = LANES // 2
    grows = SWA_GROUP * tq
    fh = _first_half_mask(tq)
    q_scale = hd ** -0.5 * LOG2E

    lane_q = lax.broadcasted_iota(jnp.int32, (tq, LANES), 1) < hd
    lane_a = lax.broadcasted_iota(jnp.int32, (WINDOW, LANES), 1) < hd
    ones = jnp.ones((2 * LANES, LANES), BF16)
    zpad = jnp.zeros((LANES - tq, LANES), BF16)
    lane_g = lax.broadcasted_iota(jnp.int32, (grows, LANES), 1)
    nt = (((1,), (1,)), ((), ()))

    def chunk(prev_refs, own_ref, idx, sl):
        if idx < n_prev:
            return prev_refs[idx][:, sl]
        return own_ref[(idx - n_prev) * tq:(idx - n_prev + 1) * tq, sl]

    for s in range(nsub):
        rs = slice(s * tq, (s + 1) * tq)
        cos_t, sin_t = cos_ref[rs, :], sin_ref[rs, :]
        if masked:
            valid_a = lane_g >= (WINDOW - (pl.program_id(1) * nsub + s) * tq)

        def keys_vals(h):
            pc, half = h // 2, h % 2
            sl = slice(pc * LANES, (pc + 1) * LANES)
            if n_prev == 1:
                ka, va = ka_refs[0][:, sl], va_refs[0][:, sl]
                kb, vb = kb_ref[rs, sl], vb_ref[rs, sl]
            else:
                ka = jnp.concatenate([chunk(ka_refs, kb_ref, s, sl), chunk(ka_refs, kb_ref, s + 1, sl)], axis=0)
                va = jnp.concatenate([chunk(va_refs, vb_ref, s, sl), chunk(va_refs, vb_ref, s + 1, sl)], axis=0)
                kb, vb = chunk(ka_refs, kb_ref, s + 2, sl), chunk(va_refs, vb_ref, s + 2, sl)
            k_all = jnp.concatenate([_dup_half(ka.astype(F32), half, lane_a).astype(BF16),
                                     _dup_half(kb.astype(F32), half, lane_q).astype(BF16), zpad], axis=0)
            v_all = jnp.concatenate([_dup_half(va.astype(F32), half, lane_a).astype(BF16),
                                     _dup_half(vb.astype(F32), half, lane_q).astype(BF16), zpad], axis=0)
            return k_all, v_all

        def queries(h):
            rows = []
            for g in range(SWA_GROUP):
                cb = h * (SWA_GROUP // 2) + g // 2
                qsl = slice(cb * LANES, (cb + 1) * LANES)
                qr = _rope_block(q_ref[rs, qsl].astype(F32), cos_t, sin_t, fh) * q_scale
                rows.append(jnp.where(lane_q == (g % 2 == 0), qr, 0.0).astype(BF16))
            return jnp.concatenate(rows, axis=0)

        def finish(h, out2):
            for j in range(SWA_GROUP // 2):
                cb = h * (SWA_GROUP // 2) + j
                osl = slice(cb * LANES, (cb + 1) * LANES)
                ob = jnp.where(lane_q, out2[(2 * j) * tq:(2 * j + 1) * tq, :],
                               out2[(2 * j + 1) * tq:(2 * j + 2) * tq, :])
                merged = (_sigmoid(za_ref[rs, osl].astype(F32)) * oa_ref[rs, osl].astype(F32)
                          + _sigmoid(zb_ref[rs, osl].astype(F32)) * ob)
                o_ref[rs, osl] = merged.astype(o_ref.dtype)

        hs_per = SWA_HEADS_PER_STAGE
        for h0 in range(0, SWA_KV_HEADS, hs_per):
            hs = range(h0, h0 + hs_per)
            kv = [keys_vals(h) for h in hs]
            lhs = [queries(h) for h in hs]
            logits = [lax.dot_general(lhs[i], kv[i][0], nt, preferred_element_type=F32) for i in range(hs_per)]
            l1 = [jnp.where(valid_a, lg[:, :LANES], -jnp.inf) if masked else lg[:, :LANES] for lg in logits]
            l2 = [logits[i][:, LANES:] + bias_ref[h0 + i] for i in range(hs_per)]
            m = [jnp.max(jnp.maximum(l1[i], l2[i]), axis=-1, keepdims=True) for i in range(hs_per)]
            p = [jnp.concatenate([jnp.exp2(l1[i] - m[i]), jnp.exp2(l2[i] - m[i])], axis=1).astype(BF16)
                 for i in range(hs_per)]
            pv = [jnp.dot(p[i], kv[i][1], preferred_element_type=F32) for i in range(hs_per)]
            den = [jnp.dot(p[i], ones, preferred_element_type=F32) for i in range(hs_per)]
            for i in range(hs_per):
                finish(h0 + i, pv[i] / den[i])


def _swa_call(proj, q_blk, k_rot, o_a, sinks, cos_t, sin_t, batch, seq, cache_k=None, cache_v=None):
    t, d = o_a.shape
    kvw = k_rot.shape[1]
    v_blk = ((q_blk + 3) * d) // kvw + 1
    if cache_k is None:
        tq = CHUNK
        nsub = SWA_BLOCKS_PER_STEP if seq % (SWA_BLOCKS_PER_STEP * tq) == 0 else 1
        nq = seq // (nsub * tq)
        rowi = lambda b, n: b * nq + n
        n_prev = WINDOW // CHUNK
        prev = lambda b, n, back: b * (seq // tq) + jnp.maximum(n * nsub - back, 0)
        ka_args = [k_rot] * n_prev
        va_args = [proj] * n_prev
        ka_specs = [pl.BlockSpec((tq, kvw), functools.partial(lambda b, n, back: (prev(b, n, back), 0),
                                                               back=n_prev - i)) for i in range(n_prev)]
        va_specs = [pl.BlockSpec((tq, kvw), functools.partial(lambda b, n, back: (prev(b, n, back), v_blk),
                                                               back=n_prev - i)) for i in range(n_prev)]
        masked = True
    else:
        tq = seq
        nsub = 1
        nq = 1
        rowi = lambda b, n: b
        n_prev = 1
        ka_args, va_args = [cache_k], [cache_v]
        ka_specs = [pl.BlockSpec((None, WINDOW, kvw), lambda b, n: (b, 0, 0))]
        va_specs = [pl.BlockSpec((None, WINDOW, kvw), lambda b, n: (b, 0, 0))]
        masked = False
    rows = nsub * tq
    wide = lambda blk: pl.BlockSpec((rows, d), lambda b, n: (rowi(b, n), blk))
    tab = pl.BlockSpec((rows, LANES), lambda b, n: (n, 0))
    lane = jnp.arange(LANES)
    sink_l2 = (sinks.astype(F32) * LOG2E).reshape(SWA_KV_HEADS, SWA_GROUP, 1, 1)
    bias_row = jnp.where(lane < tq, 0.0, jnp.where(lane == tq, sink_l2, -jnp.inf))
    sinks = jnp.broadcast_to(bias_row, (SWA_KV_HEADS, SWA_GROUP, tq, LANES)).reshape(
        SWA_KV_HEADS, SWA_GROUP * tq, LANES)
    in_specs = ([pl.BlockSpec(sinks.shape, lambda b, n: (0, 0, 0)), wide(q_blk), tab, tab] + ka_specs + va_specs +
                [pl.BlockSpec((rows, kvw), lambda b, n: (rowi(b, n), 0)),
                 pl.BlockSpec((rows, kvw), lambda b, n: (rowi(b, n), v_blk)),
                 wide(0), wide(q_blk + 1), wide(q_blk + 2)])
    args = [sinks, proj, cos_t, sin_t] + ka_args + va_args + [k_rot, proj, o_a, proj, proj]
    return pl.pallas_call(
        functools.partial(_swa_kernel, tq=tq, n_prev=n_prev, nsub=nsub, masked=masked),
        grid=(batch, nq),
        in_specs=in_specs,
        out_specs=pl.BlockSpec((rows, d), lambda b, n: (rowi(b, n), 0)),
        out_shape=jax.ShapeDtypeStruct((t, d), BF16),
        compiler_params=_cparams(("parallel", "arbitrary")),
        name="swa_merge",
    )(*args)


def _prep_weights(w_in, w_out, w_gate_up, w_down, w_gk_up, d):
    h = GLA_HEADS
    key = (d // (2 * h)) * h
    n1 = 2 * key + 2 * d
    kvw = SWA_KV_HEADS * (d // SWA_HEADS)
    w = kvw
    nb1 = n1 // w
    per_d = d // w
    q_s = list(range(per_d))
    k_s, v_s = per_d, per_d + 1
    za_s = [per_d + 2 + i for i in range(per_d)]
    zb_s = [2 * per_d + 2 + i for i in range(per_d)]
    shifted = q_s + za_s + zb_s + [k_s, v_s]
    src_blk = list(range(nb1)) + [nb1 + s for s in shifted]
    mode = [0] * nb1 + [1] * len(shifted)
    w_in_t = jnp.transpose(w_in)
    w12 = _relayout_t_call(w_in_t, w, src_blk, mode, shift=GLA_GATE_RANK, tk=2048)

    ra_cols = jnp.transpose(w_in_t[n1:n1 + GLA_GATE_RANK, :])
    w_r = jnp.pad(jnp.concatenate([ra_cols] * 3, axis=1),
                  ((0, 0), (0, LANES - 3 * GLA_GATE_RANK))).astype(BF16)
    wg_hi = w_gk_up.astype(BF16)
    wg_lo = (w_gk_up - wg_hi.astype(F32)).astype(BF16)
    wgk3 = jnp.pad(jnp.concatenate([wg_hi, wg_hi, wg_lo], axis=0),
                   ((0, LANES - 3 * GLA_GATE_RANK), (0, 0)))

    lanes_blk = list(range(w_out.shape[1] // LANES))
    job_o = (w_out, w_out.shape[0], LANES, lanes_blk, [0] * len(lanes_blk))

    dff = w_down.shape[0]
    n_src = dff // LANES
    n_tiles = -(-dff // FF_TILE)
    per_tile = FF_TILE // LANES
    src_blk, mode = [], []
    for j in range(n_tiles):
        for part in range(2):
            for sb in range(per_tile):
                blk = j * per_tile + sb
                ok = blk < n_src
                src_blk.append(part * n_src + blk if ok else 0)
                mode.append(0 if ok else 2)
    job_gu = (w_gate_up, w_gate_up.shape[0], LANES, src_blk, mode)
    cols_dn = list(range(d // 256))
    job_dn = (w_down, dff // 4, 256, cols_dn, [0] * len(cols_dn))
    return w12, w_r, wgk3, job_o, job_gu, job_dn


def _run_job(job):
    src, tr, width, src_blk, mode = job
    return _relayout_call(src, width, src_blk, mode, tr=tr)


def _layer(x, mod, pos, s0, cache_k, cache_v, weights, params, batch, seq):
    w12, w_r, wgk3, w_o, w_gu, w_dn = weights
    g_norm1, b_gk, g_gla_out, sinks, g_norm2 = params
    t, d = x.shape
    sh1, sc1, gt1, sh2, sc2, gt2 = [m.reshape(batch, 1, d) for m in jnp.split(mod, 6, axis=-1)]
    kvw = SWA_KV_HEADS * (d // SWA_HEADS)
    hd = d // SWA_HEADS
    tm = min(MM_TILE_M, t)
    q_blk = (w12.shape[1] - 3 * d - 2 * kvw) // d

    def n_blocks(job):
        return (job[0].shape[0] // job[1]) * len(job[3])

    h = _norm_call(x, g_norm1, sc1, sh1, seq, BF16)
    steps = (t // tm) * (w12.shape[1] // 1024)
    if isinstance(w_gu, tuple) and max(n_blocks(w_gu), n_blocks(w_o)) <= steps:
        proj, w_gu, w_o = _mm_call(h, w12, tm=tm, tn=1024, out_dtype=BF16, name="in_proj", side=(w_gu, w_o))
    else:
        proj = _mm_call(h, w12, tm=tm, tn=1024, out_dtype=BF16, name="in_proj")
        if isinstance(w_gu, tuple):
            w_gu, w_o = _run_job(w_gu), _run_job(w_o)
    ra3 = _mm_call(h, w_r, tm=tm, tn=LANES, name="in_proj_rank")

    o_a, s_new = _gla_call(proj, ra3, wgk3, b_gk, g_gla_out, s0, batch, seq)

    cos_t, sin_t = _rope_tables(pos, hd)
    k_blk = ((q_blk + 3) * d) // kvw
    k_rot = _rope_k_call(proj, k_blk, kvw, cos_t, sin_t, seq)
    merged = _swa_call(proj, q_blk, k_rot, o_a, sinks, cos_t, sin_t, batch, seq, cache_k, cache_v)

    x1 = _mm_call(merged, w_o, tm=tm, tn=1024, epilogue="residual", res=x, gate=gt1,
                  rows_per_group=seq, name="out_proj")
    h2 = _norm_call(x1, g_norm2, sc2, sh2, seq, BF16)
    steps = (t // tm) * (w_gu.shape[1] // (2 * FF_TILE))
    if isinstance(w_dn, tuple) and n_blocks(w_dn) <= steps:
        act, w_dn = _mm_call(h2, w_gu, tm=tm, tn=2 * FF_TILE, epilogue="swiglu", out_dtype=BF16,
                             name="ffn_gate_up", side=(w_dn,))
    else:
        act = _mm_call(h2, w_gu, tm=tm, tn=2 * FF_TILE, epilogue="swiglu", out_dtype=BF16, name="ffn_gate_up")
        if isinstance(w_dn, tuple):
            w_dn = _run_job(w_dn)
    x2 = _mm_call(act, w_dn, tm=min(MM_TILE_M // 2, t), tn=512, epilogue="residual", res=x1, gate=gt2,
                  rows_per_group=seq, name="ffn_down")

    keep = WINDOW if cache_k is None else seq
    k_keep = jnp.stack([k_rot[(b + 1) * seq - keep:(b + 1) * seq] for b in range(batch)]).reshape(
        batch, keep, SWA_KV_HEADS, hd)
    v_cols = slice((k_blk + 1) * kvw, (k_blk + 2) * kvw)
    v_new = jnp.stack([proj[(b + 1) * seq - keep:(b + 1) * seq, v_cols] for b in range(batch)])
    v_keep = v_new.astype(F32).reshape(batch, keep, SWA_KV_HEADS, hd)
    return x2, s_new, k_keep, v_keep, (w12, w_r, wgk3, w_o, w_gu, w_dn)


def kernel(x_prompt, x_sample, state_gla, cache_swa_k, cache_swa_v, c_prompt, c_sample, w_ada, b_ada,
           g_norm1, w_in, w_gk_up, b_gk, g_gla_out, swa_sinks, w_out, g_norm2, w_gate_up, w_down, g_final):
    bp, sp, d = x_prompt.shape
    bs, ss, _ = x_sample.shape
    depth = w_ada.shape[0]
    dk = w_gk_up.shape[2] // GLA_HEADS
    dv = g_gla_out.shape[1]
    kvw = SWA_KV_HEADS * (d // SWA_HEADS)

    pos_p = jnp.arange(sp)
    pos_s = PAST_LEN + jnp.arange(ss)
    s0_p = jnp.zeros((bp, GLA_HEADS, dk, dv), F32)

    nc = bp + bs
    nc_pad = -(-nc // 16) * 16
    c_all = jnp.concatenate([c_prompt, c_sample, jnp.zeros((nc_pad - nc, d), F32)], axis=0)

    hp = x_prompt.reshape(bp * sp, d)
    hs = x_sample.reshape(bs * ss, d)
    outs = [[] for _ in range(6)]
    for l in range(depth):
        mod = _mod_call(c_all, w_ada[l], b_ada[l])
        weights = _prep_weights(w_in[l], w_out[l], w_gate_up[l], w_down[l], w_gk_up[l], d)
        params = (g_norm1[l], b_gk[l], g_gla_out[l], swa_sinks[l], g_norm2[l])
        hp, s_p, k_p, v_p, weights = _layer(hp, mod[:bp], pos_p, s0_p, None, None, weights, params, bp, sp)
        hs, s_s, k_s, v_s, _ = _layer(hs, mod[bp:nc], pos_s, state_gla[l],
                                   cache_swa_k[l].reshape(bs, WINDOW, kvw),
                                   cache_swa_v[l].reshape(bs, WINDOW, kvw), weights, params, bs, ss)
        for lst, val in zip(outs, (s_p, k_p, v_p, s_s, k_s, v_s)):
            lst.append(val)
    y_prompt = _norm_call(hp, g_final, None, None, sp, F32).reshape(bp, sp, d)
    y_sample = _norm_call(hs, g_final, None, None, ss, F32).reshape(bs, ss, d)
    return (y_prompt, y_sample) + tuple(jnp.stack(o) for o in outs)
```

```python
import functools

import jax
import jax.numpy as jnp
import numpy as np
from jax import lax
from jax.experimental import pallas as pl
from jax.experimental.pallas import tpu as pltpu

F32 = jnp.float32
BF16 = jnp.bfloat16

CHUNK = 64
GLA_HEADS = 4
GLA_GATE_RANK = 16
GLA_GATE_NORM = 16.0
SWA_HEADS = 64
SWA_KV_HEADS = 8
SWA_GROUP = SWA_HEADS // SWA_KV_HEADS
WINDOW = 128
ROPE_THETA = 10000.0
NORM_EPS = 1e-6
PAST_LEN = 2048
LOG2E = 1.4426950408889634

LANES = 128
VMEM_LIMIT_BYTES = 56 * 1024 * 1024
VMEM_LIMIT_BYTES_MAX = 62 * 1024 * 1024
MM_TILE_M = 1024
GLA_HEADS_PER_STEP = 4
GLA_BLOCK_CHUNKS = 4
SWA_BLOCKS_PER_STEP = 2
SWA_HEADS_PER_STAGE = 4
FF_TILE = 512


def _cparams(sem, vmem=VMEM_LIMIT_BYTES):
    return pltpu.CompilerParams(dimension_semantics=sem, vmem_limit_bytes=vmem)


def _silu(x):
    return x * (1.0 / (1.0 + jnp.exp(-x)))


def _sigmoid(x):
    return 1.0 / (1.0 + jnp.exp(-x))


def _pick(n, prefs):
    for p in prefs:
        if n % p == 0:
            return p
    return n


def _relayout_kernel(src_ref, mode_ref, main, o_ref):
    mode = mode_ref[pl.program_id(1)]

    @pl.when(mode == 0)
    def _():
        o_ref[...] = main[...].astype(BF16)

    @pl.when(mode == 2)
    def _():
        o_ref[...] = jnp.zeros(o_ref.shape, BF16)


def _relayout_call(src, width, src_blk, mode, tr=None):
    r = src.shape[0]
    tr = r if tr is None else tr
    nb = len(src_blk)
    grid_spec = pltpu.PrefetchScalarGridSpec(
        num_scalar_prefetch=2,
        grid=(r // tr, nb),
        in_specs=[pl.BlockSpec((tr, width), lambda i, b, s, m: (i, s[b]))],
        out_specs=pl.BlockSpec((tr, width), lambda i, b, s, m: (i, b)),
    )
    return pl.pallas_call(
        _relayout_kernel,
        grid_spec=grid_spec,
        out_shape=jax.ShapeDtypeStruct((r, nb * width), BF16),
        compiler_params=_cparams(("parallel", "arbitrary")),
        name="weight_relayout",
    )(jnp.asarray(np.asarray(src_blk, np.int32)), jnp.asarray(np.asarray(mode, np.int32)), src)


def _relayout_t_kernel(src_ref, nxt_ref, mode_ref, main, nxt, o_ref, *, shift):
    mode = mode_ref[pl.program_id(1)]
    w = o_ref.shape[1]

    @pl.when(mode == 0)
    def _():
        o_ref[...] = jnp.transpose(main[...]).astype(BF16)

    @pl.when(mode == 1)
    def _():
        x = jnp.concatenate([main[...], nxt[...]], axis=0)
        o_ref[...] = jnp.transpose(x[shift:shift + w, :]).astype(BF16)


def _relayout_t_call(src_t, width, src_blk, mode, shift, tk):
    r = src_t.shape[1]
    nb = len(src_blk)
    src_blk = np.asarray(src_blk, np.int32)
    nxt_blk = (src_blk + 1) * (width // shift)
    grid_spec = pltpu.PrefetchScalarGridSpec(
        num_scalar_prefetch=3,
        grid=(r // tk, nb),
        in_specs=[pl.BlockSpec((width, tk), lambda i, b, s, nx, m: (s[b], i)),
                  pl.BlockSpec((shift, tk), lambda i, b, s, nx, m: (nx[b], i))],
        out_specs=pl.BlockSpec((tk, width), lambda i, b, s, nx, m: (i, b)),
    )
    return pl.pallas_call(
        functools.partial(_relayout_t_kernel, shift=shift),
        grid_spec=grid_spec,
        out_shape=jax.ShapeDtypeStruct((r, nb * width), BF16),
        compiler_params=_cparams(("parallel", "arbitrary")),
        name="weight_relayout_t",
    )(jnp.asarray(src_blk), jnp.asarray(nxt_blk, jnp.int32), jnp.asarray(np.asarray(mode, np.int32)),
      src_t, src_t)


def _mod_kernel(c_ref, w_ref, b_ref, o_ref):
    a = _silu(c_ref[...]).astype(BF16)
    w = w_ref[...].astype(BF16)
    o_ref[...] = jnp.dot(a, w, preferred_element_type=F32) + b_ref[...]


def _mod_call(c_all, w_ada, b_ada):
    m, d = c_all.shape
    n = w_ada.shape[1]
    tn = _pick(n, (512, 256, 128))
    return pl.pallas_call(
        _mod_kernel,
        grid=(n // tn,),
        in_specs=[pl.BlockSpec((m, d), lambda j: (0, 0)),
                  pl.BlockSpec((d, tn), lambda j: (0, j)),
                  pl.BlockSpec((1, tn), lambda j: (0, j))],
        out_specs=pl.BlockSpec((m, tn), lambda j: (0, j)),
        out_shape=jax.ShapeDtypeStruct((m, n), F32),
        compiler_params=_cparams(("arbitrary",)),
        name="adaln_mod",
    )(c_all, w_ada, b_ada.reshape(1, n))


def _norm_kernel(x_ref, g_ref, *rest, modulate):
    o_ref = rest[-1]
    x = x_ref[...]
    y = x * lax.rsqrt(jnp.mean(x * x, axis=-1, keepdims=True) + NORM_EPS) * g_ref[...]
    if modulate:
        sc_ref, sh_ref = rest[0], rest[1]
        y = y * (1.0 + sc_ref[...]) + sh_ref[...]
    o_ref[...] = y.astype(o_ref.dtype)


def _norm_call(x, g, sc, sh, rows_per_group, out_dtype):
    t, d = x.shape
    tr = _pick(rows_per_group, (256, 128, 64, 32, 16, 8))
    tiles = rows_per_group // tr
    modulate = sc is not None
    in_specs = [pl.BlockSpec((tr, d), lambda i: (i, 0)), pl.BlockSpec((1, d), lambda i: (0, 0))]
    args = [x, g.reshape(1, d)]
    if modulate:
        in_specs += [pl.BlockSpec((None, 1, d), lambda i: (i // tiles, 0, 0))] * 2
        args += [sc, sh]
    return pl.pallas_call(
        functools.partial(_norm_kernel, modulate=modulate),
        grid=(t // tr,),
        in_specs=in_specs,
        out_specs=pl.BlockSpec((tr, d), lambda i: (i, 0)),
        out_shape=jax.ShapeDtypeStruct((t, d), out_dtype),
        compiler_params=_cparams(("parallel",)),
        name="rmsnorm_mod",
    )(*args)


def _mm_kernel(*refs, epilogue, n_side, side_blocks, nj):
    tabs = refs[:2 * n_side]
    a_ref, w_ref = refs[2 * n_side:2 * n_side + 2]
    rest = refs[2 * n_side + 2:]
    n_extra = 2 if epilogue == "residual" else 0
    extras = rest[:n_extra]
    side_in = rest[n_extra:n_extra + n_side]
    o_ref = rest[n_extra + n_side]
    side_out = rest[n_extra + n_side + 1:]

    acc = jnp.dot(a_ref[...], w_ref[...], preferred_element_type=F32)
    if epilogue == "residual":
        res_ref, gate_ref = extras
        out = res_ref[...] + gate_ref[...] * acc
    elif epilogue == "swiglu":
        half = acc.shape[1] // 2
        out = _silu(acc[:, :half]) * acc[:, half:]
    else:
        out = acc
    o_ref[...] = out.astype(o_ref.dtype)

    t = pl.program_id(0) * nj + pl.program_id(1)
    for q in range(n_side):
        nrt, nb = side_blocks[q]

        @pl.when(t < nrt * nb)
        def _(q=q, nb=nb):
            mode = tabs[2 * q + 1][t % nb]
            side_out[q][...] = jnp.where(mode == 2, 0.0, side_in[q][...]).astype(BF16)


def _mm_call(a, w, *, tm, tn, epilogue="plain", res=None, gate=None, rows_per_group=None,
             out_dtype=F32, name="matmul", side=(), vmem=VMEM_LIMIT_BYTES):
    m = a.shape[0]
    kdim, n = w.shape
    n_out = n // 2 if epilogue == "swiglu" else n
    tn_out = tn // 2 if epilogue == "swiglu" else tn
    ni, nj = m // tm, n // tn
    n_side = len(side)
    in_specs = [pl.BlockSpec((tm, kdim), lambda i, j, *_: (i, 0)),
                pl.BlockSpec((kdim, tn), lambda i, j, *_: (0, j))]
    args = [a, w]
    if epilogue == "residual":
        in_specs.append(pl.BlockSpec((tm, tn), lambda i, j, *_: (i, j)))
        if rows_per_group % tm == 0:
            tiles = rows_per_group // tm
            in_specs.append(pl.BlockSpec((None, 1, tn), lambda i, j, *_: (i // tiles, 0, j)))
        else:
            gate = jnp.repeat(gate[:, 0, :], rows_per_group, axis=0)
            in_specs.append(pl.BlockSpec((tm, tn), lambda i, j, *_: (i, j)))
        args += [res, gate]
    out_specs = [pl.BlockSpec((tm, tn_out), lambda i, j, *_: (i, j))]
    out_shape = [jax.ShapeDtypeStruct((m, n_out), out_dtype)]
    tabs, side_blocks = [], []
    for q, (src, tr, width, src_blk, mode) in enumerate(side):
        nrt, nb = src.shape[0] // tr, len(src_blk)
        assert nrt * nb <= ni * nj, "not enough grid steps for this side job"
        side_blocks.append((nrt, nb))
        tabs += [jnp.asarray(np.asarray(src_blk, np.int32)), jnp.asarray(np.asarray(mode, np.int32))]

        def src_map(i, j, *t, q=q, nrt=nrt, nb=nb):
            step = jnp.minimum(i * nj + j, nrt * nb - 1)
            return (step // nb, t[2 * q][step % nb])

        def dst_map(i, j, *t, nrt=nrt, nb=nb):
            step = jnp.minimum(i * nj + j, nrt * nb - 1)
            return (step // nb, step % nb)

        in_specs.append(pl.BlockSpec((tr, width), src_map))
        args.append(src)
        out_specs.append(pl.BlockSpec((tr, width), dst_map))
        out_shape.append(jax.ShapeDtypeStruct((src.shape[0], nb * width), BF16))
    grid_spec = pltpu.PrefetchScalarGridSpec(num_scalar_prefetch=2 * n_side, grid=(ni, nj),
                                            in_specs=in_specs, out_specs=out_specs)
    outs = pl.pallas_call(
        functools.partial(_mm_kernel, epilogue=epilogue, n_side=n_side, side_blocks=tuple(side_blocks), nj=nj),
        grid_spec=grid_spec,
        out_shape=out_shape,
        compiler_params=_cparams(("arbitrary", "arbitrary") if n_side else ("parallel", "parallel"), vmem),
        name=name,
    )(*tabs, *args)
    return outs if n_side else outs[0]


def _rope_tables(pos, head_dim):
    half = head_dim // 2
    inv = ROPE_THETA ** (-jnp.arange(half, dtype=F32) / half)
    ang = pos.astype(F32)[:, None] * inv[None, :]
    cos, sin = jnp.cos(ang), jnp.sin(ang)
    reps = LANES // head_dim
    cos_t = jnp.tile(jnp.concatenate([cos, cos], axis=1), (1, reps))
    sin_t = jnp.tile(jnp.concatenate([-sin, sin], axis=1), (1, reps))
    return cos_t, sin_t


def _rope_block(x, cos_t, sin_t, first_half):
    partner = jnp.where(first_half, pltpu.roll(x, 96, 1), pltpu.roll(x, 32, 1))
    return x * cos_t + partner * sin_t


def _first_half_mask(rows):
    lane = lax.broadcasted_iota(jnp.int32, (rows, LANES), 1)
    return (lane % 64) < 32


def _rope_k_kernel(k_ref, cos_ref, sin_ref, o_ref):
    rows, width = k_ref.shape
    fh = _first_half_mask(rows)
    cos_t, sin_t = cos_ref[...], sin_ref[...]
    for c in range(width // LANES):
        sl = slice(c * LANES, (c + 1) * LANES)
        o_ref[:, sl] = _rope_block(k_ref[:, sl].astype(F32), cos_t, sin_t, fh)


def _rope_k_call(proj, col_block, width, cos_t, sin_t, rows_per_group):
    t = proj.shape[0]
    tr = _pick(rows_per_group, (1024, 512, 256, 128, 64, 32, 16, 8))
    tiles = rows_per_group // tr
    return pl.pallas_call(
        _rope_k_kernel,
        grid=(t // tr,),
        in_specs=[pl.BlockSpec((tr, width), lambda i: (i, col_block)),
                  pl.BlockSpec((tr, LANES), lambda i: (i % tiles, 0)),
                  pl.BlockSpec((tr, LANES), lambda i: (i % tiles, 0))],
        out_specs=pl.BlockSpec((tr, width), lambda i: (i, 0)),
        out_shape=jax.ShapeDtypeStruct((t, width), F32),
        compiler_params=_cparams(("parallel",)),
        name="rope_k",
    )(proj, cos_t, sin_t)


def _gla_kernel(q_ref, k_ref, v_ref, ga_ref, ra_ref, wgk_ref, bgk_ref, g_ref, s0_ref,
                o_ref, sout_ref, s_scr, *, n_steps, c, scale, hp):
    n = pl.program_id(2)
    rows = q_ref.shape[0]
    dk = q_ref.shape[1] // hp
    dv = v_ref.shape[1] // hp
    nr = rows // c
    nt = (((1,), (1,)), ((), ()))
    tn = (((0,), (0,)), ((), ()))
    heads = range(hp)
    ksl = lambda u: slice(u * dk, (u + 1) * dk)
    vsl = lambda u: slice(u * dv, (u + 1) * dv)

    @pl.when(n == 0)
    def _():
        s_scr[...] = s0_ref[...]

    ra = ra_ref[...]
    lane = lax.broadcasted_iota(jnp.int32, ra.shape, 1)
    ra_hi = ra.astype(BF16)
    ra_lo = (ra - ra_hi.astype(F32)).astype(BF16)
    lhs = jnp.where(jnp.logical_and(lane >= GLA_GATE_RANK, lane < 2 * GLA_GATE_RANK), ra_lo, ra_hi)
    z = jnp.dot(lhs, wgk_ref[...], preferred_element_type=F32) + bgk_ref[...]
    log_a = (jnp.minimum(z, 0.0) - jnp.log(1.0 + jnp.exp(-jnp.abs(z)))) * (1.0 / GLA_GATE_NORM)

    ri = lax.broadcasted_iota(jnp.int32, (rows, 2 * rows), 0)
    ci = lax.broadcasted_iota(jnp.int32, (rows, 2 * rows), 1) % rows
    tril2 = jnp.where(jnp.logical_and(ri // c == ci // c, ci <= ri), 1.0, 0.0).astype(BF16)
    la_hi = log_a.astype(BF16)
    la_lo = (log_a - la_hi.astype(F32)).astype(BF16)
    b_all = jnp.dot(tril2, jnp.concatenate([la_hi, la_lo], axis=0), preferred_element_type=F32)

    row = lax.broadcasted_iota(jnp.int32, (c, c), 0)
    col = lax.broadcasted_iota(jnp.int32, (c, c), 1)
    causal = row >= col

    vs, qds, kds, kd32, o_parts = [], [], [], [], []
    cum = [None]
    for r in range(nr):
        sl = slice(r * c, (r + 1) * c)
        b = b_all[sl, :]
        b_mid = b[c // 2:c // 2 + 1, :]
        b_last = b[c - 1:c, :]
        q = q_ref[sl, :].astype(F32) * scale
        k = k_ref[sl, :].astype(F32)
        v = v_ref[sl, :]
        qs = (q * jnp.exp(b - b_mid)).astype(BF16)
        ks = (k * jnp.exp(b_mid - b)).astype(BF16)
        sc = [lax.dot_general(qs[:, ksl(u)], ks[:, ksl(u)], nt, preferred_element_type=F32) for u in heads]
        o = [jnp.dot(jnp.where(causal, sc[u], 0.0).astype(BF16), v[:, vsl(u)], preferred_element_type=F32)
             for u in heads]
        qd = q * jnp.exp(b)
        for rp in range(r):
            fac = cum[r] - cum[rp + 1] if rp + 1 < r else None
            qx = (qd if fac is None else qd * jnp.exp(fac)).astype(BF16)
            sx = [lax.dot_general(qx[:, ksl(u)], kds[rp][:, ksl(u)], nt, preferred_element_type=F32)
                  for u in heads]
            o = [o[u] + jnp.dot(sx[u].astype(BF16), vs[rp][:, vsl(u)], preferred_element_type=F32)
                 for u in heads]
        qds.append(qd if cum[r] is None else qd * jnp.exp(cum[r]))
        kd = k * jnp.exp(b_last - b)
        kd32.append(kd)
        kds.append(kd.astype(BF16))
        vs.append(v)
        o_parts.append(o)
        cum.append(b_last if cum[r] is None else cum[r] + b_last)

    q_all = jnp.concatenate(qds, axis=0).astype(BF16) if nr > 1 else qds[0].astype(BF16)
    total = cum[nr]
    k_parts = []
    for r in range(nr):
        if r + 1 < nr:
            k_parts.append((kd32[r] * jnp.exp(total - cum[r + 1])).astype(BF16))
        else:
            k_parts.append(kds[r])
    k_all = jnp.concatenate(k_parts, axis=0) if nr > 1 else k_parts[0]
    dec = jnp.exp(total)
    wblk = 2 * LANES
    dec_w = []
    for u in heads:
        dec_col = jnp.transpose(jnp.broadcast_to(dec[:, ksl(u)], (LANES, dk)))
        dec_w.append(jnp.concatenate([dec_col] * (wblk // LANES), axis=1))
    o_inter = [[] for _ in heads]
    for j in range(dv // wblk):
        for u in heads:
            sl = slice(j * wblk, (j + 1) * wblk)
            vcol = slice(u * dv + j * wblk, u * dv + (j + 1) * wblk)
            s_blk = s_scr[u, :, sl]
            o_inter[u].append(jnp.dot(q_all[:, ksl(u)], s_blk.astype(BF16), preferred_element_type=F32))
            upd = lax.dot_general(k_all[:, ksl(u)], v_ref[:, vcol], tn, preferred_element_type=F32)
            s_scr[u, :, sl] = s_blk * dec_w[u] + upd

    g = g_ref[...]
    for r in range(nr):
        sl = slice(r * c, (r + 1) * c)
        for u in heads:
            o = o_parts[r][u] + jnp.concatenate([p[sl, :] for p in o_inter[u]], axis=1)
            on = o * lax.rsqrt(jnp.mean(o * o, axis=-1, keepdims=True) + NORM_EPS) * g
            o_ref[sl, vsl(u)] = (on * _silu(ga_ref[sl, vsl(u)].astype(F32))).astype(o_ref.dtype)

    @pl.when(n == n_steps - 1)
    def _():
        sout_ref[...] = s_scr[...]


def _gla_call(proj, ra3, wgk3, b_gk, g_gla_out, s0, batch, seq):
    h = GLA_HEADS
    hp = GLA_HEADS_PER_STEP
    hg = h // hp
    dk = wgk3.shape[1] // h
    dv = g_gla_out.shape[0]
    c = min(CHUNK, seq)
    rows = c * GLA_BLOCK_CHUNKS if seq % (c * GLA_BLOCK_CHUNKS) == 0 else c
    ns = seq // rows
    t = batch * seq
    wk, wv = hp * dk, hp * dv
    k_off = (h * dk) // wk
    v_off = (2 * h * dk) // wv
    g_off = v_off + hg
    row = lambda b, hh, n: b * ns + n
    state_mode = dict(pipeline_mode=pl.Buffered(1)) if ns > 1 else {}
    kern = functools.partial(_gla_kernel, n_steps=ns, c=c, scale=dk ** -0.5, hp=hp)
    return pl.pallas_call(
        kern,
        grid=(batch, hg, ns),
        in_specs=[
            pl.BlockSpec((rows, wk), lambda b, hh, n: (row(b, hh, n), hh)),
            pl.BlockSpec((rows, wk), lambda b, hh, n: (row(b, hh, n), k_off + hh)),
            pl.BlockSpec((rows, wv), lambda b, hh, n: (row(b, hh, n), v_off + hh)),
            pl.BlockSpec((rows, wv), lambda b, hh, n: (row(b, hh, n), g_off + hh)),
            pl.BlockSpec((rows, LANES), lambda b, hh, n: (row(b, hh, n), 0)),
            pl.BlockSpec((LANES, wk), lambda b, hh, n: (0, hh)),
            pl.BlockSpec((1, wk), lambda b, hh, n: (0, hh)),
            pl.BlockSpec((1, dv), lambda b, hh, n: (0, 0)),
            pl.BlockSpec((None, hp, dk, dv), lambda b, hh, n: (b, hh, 0, 0), **state_mode),
        ],
        out_specs=[
            pl.BlockSpec((rows, wv), lambda b, hh, n: (row(b, hh, n), hh)),
            pl.BlockSpec((None, hp, dk, dv), lambda b, hh, n: (b, hh, 0, 0), **state_mode),
        ],
        out_shape=[jax.ShapeDtypeStruct((t, h * dv), BF16),
                   jax.ShapeDtypeStruct((batch, h, dk, dv), F32)],
        scratch_shapes=[pltpu.VMEM((hp, dk, dv), F32)],
        compiler_params=_cparams(("parallel", "parallel", "arbitrary")),
        name="gla_chunked",
    )(proj, proj, proj, proj, ra3, wgk3, b_gk.reshape(1, -1), g_gla_out.reshape(1, -1), s0)


def _dup_half(x, half, lane_lo):
    rolled = pltpu.roll(x, 64, 1)
    return jnp.where(lane_lo == (half == 0), x, rolled)


def _swa_kernel(*refs, tq, n_prev, nsub, masked):
    bias_ref, q_ref, cos_ref, sin_ref = refs[0:4]
    ka_refs = refs[4:4 + n_prev]
    va_refs = refs[4 + n_prev:4 + 2 * n_prev]
    kb_ref, vb_ref, oa_ref, za_ref, zb_ref, o_ref = refs[4 + 2 * n_prev:]

    hd = LANES // 2
    grows = SWA_GROUP * tq
    fh = _first_half_mask(tq)
    q_scale = hd ** -0.5 * LOG2E

    lane_q = lax.broadcasted_iota(jnp.int32, (tq, LANES), 1) < hd
    lane_a = lax.broadcasted_iota(jnp.int32, (WINDOW, LANES), 1) < hd
    ones = jnp.ones((2 * LANES, LANES), BF16)
    zpad = jnp.zeros((LANES - tq, LANES), BF16)
    lane_g = lax.broadcasted_iota(jnp.int32, (grows, LANES), 1)
    nt = (((1,), (1,)), ((), ()))

    def chunk(prev_refs, own_ref, idx, sl):
        if idx < n_prev:
            return prev_refs[idx][:, sl]
        return own_ref[(idx - n_prev) * tq:(idx - n_prev + 1) * tq, sl]

    for s in range(nsub):
        rs = slice(s * tq, (s + 1) * tq)
        cos_t, sin_t = cos_ref[rs, :], sin_ref[rs, :]
        if masked:
            valid_a = lane_g >= (WINDOW - (pl.program_id(1) * nsub + s) * tq)

        def keys_vals(h):
            pc, half = h // 2, h % 2
            sl = slice(pc * LANES, (pc + 1) * LANES)
            if n_prev == 1:
                ka, va = ka_refs[0][:, sl], va_refs[0][:, sl]
                kb, vb = kb_ref[rs, sl], vb_ref[rs, sl]
            else:
                ka = jnp.concatenate([chunk(ka_refs, kb_ref, s, sl), chunk(ka_refs, kb_ref, s + 1, sl)], axis=0)
                va = jnp.concatenate([chunk(va_refs, vb_ref, s, sl), chunk(va_refs, vb_ref, s + 1, sl)], axis=0)
                kb, vb = chunk(ka_refs, kb_ref, s + 2, sl), chunk(va_refs, vb_ref, s + 2, sl)
            k_all = jnp.concatenate([_dup_half(ka.astype(F32), half, lane_a).astype(BF16),
                                     _dup_half(kb.astype(F32), half, lane_q).astype(BF16), zpad], axis=0)
            v_all = jnp.concatenate([_dup_half(va.astype(F32), half, lane_a).astype(BF16),
                                     _dup_half(vb.astype(F32), half, lane_q).astype(BF16), zpad], axis=0)
            return k_all, v_all

        def queries(h):
            rows = []
            for g in range(SWA_GROUP):
                cb = h * (SWA_GROUP // 2) + g // 2
                qsl = slice(cb * LANES, (cb + 1) * LANES)
                qr = _rope_block(q_ref[rs, qsl].astype(F32), cos_t, sin_t, fh) * q_scale
                rows.append(jnp.where(lane_q == (g % 2 == 0), qr, 0.0).astype(BF16))
            return jnp.concatenate(rows, axis=0)

        def finish(h, out2):
            for j in range(SWA_GROUP // 2):
                cb = h * (SWA_GROUP // 2) + j
                osl = slice(cb * LANES, (cb + 1) * LANES)
                ob = jnp.where(lane_q, out2[(2 * j) * tq:(2 * j + 1) * tq, :],
                               out2[(2 * j + 1) * tq:(2 * j + 2) * tq, :])
                merged = (_sigmoid(za_ref[rs, osl].astype(F32)) * oa_ref[rs, osl].astype(F32)
                          + _sigmoid(zb_ref[rs, osl].astype(F32)) * ob)
                o_ref[rs, osl] = merged.astype(o_ref.dtype)

        hs_per = SWA_HEADS_PER_STAGE
        for h0 in range(0, SWA_KV_HEADS, hs_per):
            hs = range(h0, h0 + hs_per)
            kv = [keys_vals(h) for h in hs]
            lhs = [queries(h) for h in hs]
            logits = [lax.dot_general(lhs[i], kv[i][0], nt, preferred_element_type=F32) for i in range(hs_per)]
            l1 = [jnp.where(valid_a, lg[:, :LANES], -jnp.inf) if masked else lg[:, :LANES] for lg in logits]
            l2 = [logits[i][:, LANES:] + bias_ref[h0 + i] for i in range(hs_per)]
            m = [jnp.max(jnp.maximum(l1[i], l2[i]), axis=-1, keepdims=True) for i in range(hs_per)]
            p = [jnp.concatenate([jnp.exp2(l1[i] - m[i]), jnp.exp2(l2[i] - m[i])], axis=1).astype(BF16)
                 for i in range(hs_per)]
            pv = [jnp.dot(p[i], kv[i][1], preferred_element_type=F32) for i in range(hs_per)]
            den = [jnp.dot(p[i], ones, preferred_element_type=F32) for i in range(hs_per)]
            for i in range(hs_per):
                finish(h0 + i, pv[i] / den[i])


def _swa_call(proj, q_blk, k_rot, o_a, sinks, cos_t, sin_t, batch, seq, cache_k=None, cache_v=None):
    t, d = o_a.shape
    kvw = k_rot.shape[1]
    v_blk = ((q_blk + 3) * d) // kvw + 1
    if cache_k is None:
        tq = CHUNK
        nsub = SWA_BLOCKS_PER_STEP if seq % (SWA_BLOCKS_PER_STEP * tq) == 0 else 1
        nq = seq // (nsub * tq)
        rowi = lambda b, n: b * nq + n
        n_prev = WINDOW // CHUNK
        prev = lambda b, n, back: b * (seq // tq) + jnp.maximum(n * nsub - back, 0)
        ka_args = [k_rot] * n_prev
        va_args = [proj] * n_prev
        ka_specs = [pl.BlockSpec((tq, kvw), functools.partial(lambda b, n, back: (prev(b, n, back), 0),
                                                               back=n_prev - i)) for i in range(n_prev)]
        va_specs = [pl.BlockSpec((tq, kvw), functools.partial(lambda b, n, back: (prev(b, n, back), v_blk),
                                                               back=n_prev - i)) for i in range(n_prev)]
        masked = True
    else:
        tq = seq
        nsub = 1
        nq = 1
        rowi = lambda b, n: b
        n_prev = 1
        ka_args, va_args = [cache_k], [cache_v]
        ka_specs = [pl.BlockSpec((None, WINDOW, kvw), lambda b, n: (b, 0, 0))]
        va_specs = [pl.BlockSpec((None, WINDOW, kvw), lambda b, n: (b, 0, 0))]
        masked = False
    rows = nsub * tq
    wide = lambda blk: pl.BlockSpec((rows, d), lambda b, n: (rowi(b, n), blk))
    tab = pl.BlockSpec((rows, LANES), lambda b, n: (n, 0))
    lane = jnp.arange(LANES)
    sink_l2 = (sinks.astype(F32) * LOG2E).reshape(SWA_KV_HEADS, SWA_GROUP, 1, 1)
    bias_row = jnp.where(lane < tq, 0.0, jnp.where(lane == tq, sink_l2, -jnp.inf))
    sinks = jnp.broadcast_to(bias_row, (SWA_KV_HEADS, SWA_GROUP, tq, LANES)).reshape(
        SWA_KV_HEADS, SWA_GROUP * tq, LANES)
    in_specs = ([pl.BlockSpec(sinks.shape, lambda b, n: (0, 0, 0)), wide(q_blk), tab, tab] + ka_specs + va_specs +
                [pl.BlockSpec((rows, kvw), lambda b, n: (rowi(b, n), 0)),
                 pl.BlockSpec((rows, kvw), lambda b, n: (rowi(b, n), v_blk)),
                 wide(0), wide(q_blk + 1), wide(q_blk + 2)])
    args = [sinks, proj, cos_t, sin_t] + ka_args + va_args + [k_rot, proj, o_a, proj, proj]
    return pl.pallas_call(
        functools.partial(_swa_kernel, tq=tq, n_prev=n_prev, nsub=nsub, masked=masked),
        grid=(batch, nq),
        in_specs=in_specs,
        out_specs=pl.BlockSpec((rows, d), lambda b, n: (rowi(b, n), 0)),
        out_shape=jax.ShapeDtypeStruct((t, d), BF16),
        compiler_params=_cparams(("parallel", "arbitrary")),
        name="swa_merge",
    )(*args)


def _prep_weights(w_in, w_out, w_gate_up, w_down, w_gk_up, d):
    h = GLA_HEADS
    key = (d // (2 * h)) * h
    n1 = 2 * key + 2 * d
    kvw = SWA_KV_HEADS * (d // SWA_HEADS)
    w = kvw
    nb1 = n1 // w
    per_d = d // w
    q_s = list(range(per_d))
    k_s, v_s = per_d, per_d + 1
    za_s = [per_d + 2 + i for i in range(per_d)]
    zb_s = [2 * per_d + 2 + i for i in range(per_d)]
    shifted = q_s + za_s + zb_s + [k_s, v_s]
    src_blk = list(range(nb1)) + [nb1 + s for s in shifted]
    mode = [0] * nb1 + [1] * len(shifted)
    w_in_t = jnp.transpose(w_in)
    w12 = _relayout_t_call(w_in_t, w, src_blk, mode, shift=GLA_GATE_RANK, tk=2048)

    ra_cols = jnp.transpose(w_in_t[n1:n1 + GLA_GATE_RANK, :])
    w_r = jnp.pad(jnp.concatenate([ra_cols] * 3, axis=1),
                  ((0, 0), (0, LANES - 3 * GLA_GATE_RANK))).astype(BF16)
    wg_hi = w_gk_up.astype(BF16)
    wg_lo = (w_gk_up - wg_hi.astype(F32)).astype(BF16)
    wgk3 = jnp.pad(jnp.concatenate([wg_hi, wg_hi, wg_lo], axis=0),
                   ((0, LANES - 3 * GLA_GATE_RANK), (0, 0)))

    lanes_blk = list(range(w_out.shape[1] // LANES))
    job_o = (w_out, w_out.shape[0], LANES, lanes_blk, [0] * len(lanes_blk))

    dff = w_down.shape[0]
    n_src = dff // LANES
    n_tiles = -(-dff // FF_TILE)
    per_tile = FF_TILE // LANES
    src_blk, mode = [], []
    for j in range(n_tiles):
        for part in range(2):
            for sb in range(per_tile):
                blk = j * per_tile + sb
                ok = blk < n_src
                src_blk.append(part * n_src + blk if ok else 0)
                mode.append(0 if ok else 2)
    job_gu = (w_gate_up, w_gate_up.shape[0], LANES, src_blk, mode)
    cols_dn = list(range(d // 256))
    job_dn = (w_down, dff // 4, 256, cols_dn, [0] * len(cols_dn))
    return w12, w_r, wgk3, job_o, job_gu, job_dn


def _run_job(job):
    src, tr, width, src_blk, mode = job
    return _relayout_call(src, width, src_blk, mode, tr=tr)


def _layer(x, mod, pos, s0, cache_k, cache_v, weights, params, batch, seq):
    w12, w_r, wgk3, w_o, w_gu, w_dn = weights
    g_norm1, b_gk, g_gla_out, sinks, g_norm2 = params
    t, d = x.shape
    sh1, sc1, gt1, sh2, sc2, gt2 = [m.reshape(batch, 1, d) for m in jnp.split(mod, 6, axis=-1)]
    kvw = SWA_KV_HEADS * (d // SWA_HEADS)
    hd = d // SWA_HEADS
    tm = min(MM_TILE_M, t)
    q_blk = (w12.shape[1] - 3 * d - 2 * kvw) // d

    def n_blocks(job):
        return (job[0].shape[0] // job[1]) * len(job[3])

    h = _norm_call(x, g_norm1, sc1, sh1, seq, BF16)
    steps = (t // tm) * (w12.shape[1] // 1024)
    if isinstance(w_gu, tuple) and max(n_blocks(w_gu), n_blocks(w_o)) <= steps:
        proj, w_gu, w_o = _mm_call(h, w12, tm=tm, tn=1024, out_dtype=BF16, name="in_proj", side=(w_gu, w_o))
    else:
        proj = _mm_call(h, w12, tm=tm, tn=1024, out_dtype=BF16, name="in_proj")
        if isinstance(w_gu, tuple):
            w_gu, w_o = _run_job(w_gu), _run_job(w_o)
    ra3 = _mm_call(h, w_r, tm=tm, tn=LANES, name="in_proj_rank")

    o_a, s_new = _gla_call(proj, ra3, wgk3, b_gk, g_gla_out, s0, batch, seq)

    cos_t, sin_t = _rope_tables(pos, hd)
    k_blk = ((q_blk + 3) * d) // kvw
    k_rot = _rope_k_call(proj, k_blk, kvw, cos_t, sin_t, seq)
    merged = _swa_call(proj, q_blk, k_rot, o_a, sinks, cos_t, sin_t, batch, seq, cache_k, cache_v)

    x1 = _mm_call(merged, w_o, tm=tm, tn=1024, epilogue="residual", res=x, gate=gt1,
                  rows_per_group=seq, name="out_proj")
    h2 = _norm_call(x1, g_norm2, sc2, sh2, seq, BF16)
    steps = (t // tm) * (w_gu.shape[1] // (2 * FF_TILE))
    if isinstance(w_dn, tuple) and n_blocks(w_dn) <= steps:
        act, w_dn = _mm_call(h2, w_gu, tm=tm, tn=2 * FF_TILE, epilogue="swiglu", out_dtype=BF16,
                             name="ffn_gate_up", side=(w_dn,))
    else:
        act = _mm_call(h2, w_gu, tm=tm, tn=2 * FF_TILE, epilogue="swiglu", out_dtype=BF16, name="ffn_gate_up")
        if isinstance(w_dn, tuple):
            w_dn = _run_job(w_dn)
    x2 = _mm_call(act, w_dn, tm=tm, tn=256, epilogue="residual", res=x1, gate=gt2,
                  rows_per_group=seq, name="ffn_down", vmem=VMEM_LIMIT_BYTES_MAX)

    keep = WINDOW if cache_k is None else seq
    k_keep = jnp.stack([k_rot[(b + 1) * seq - keep:(b + 1) * seq] for b in range(batch)]).reshape(
        batch, keep, SWA_KV_HEADS, hd)
    v_cols = slice((k_blk + 1) * kvw, (k_blk + 2) * kvw)
    v_new = jnp.stack([proj[(b + 1) * seq - keep:(b + 1) * seq, v_cols] for b in range(batch)])
    v_keep = v_new.astype(F32).reshape(batch, keep, SWA_KV_HEADS, hd)
    return x2, s_new, k_keep, v_keep, (w12, w_r, wgk3, w_o, w_gu, w_dn)


def kernel(x_prompt, x_sample, state_gla, cache_swa_k, cache_swa_v, c_prompt, c_sample, w_ada, b_ada,
           g_norm1, w_in, w_gk_up, b_gk, g_gla_out, swa_sinks, w_out, g_norm2, w_gate_up, w_down, g_final):
    bp, sp, d = x_prompt.shape
    bs, ss, _ = x_sample.shape
    depth = w_ada.shape[0]
    dk = w_gk_up.shape[2] // GLA_HEADS
    dv = g_gla_out.shape[1]
    kvw = SWA_KV_HEADS * (d // SWA_HEADS)

    pos_p = jnp.arange(sp)
    pos_s = PAST_LEN + jnp.arange(ss)
    s0_p = jnp.zeros((bp, GLA_HEADS, dk, dv), F32)

    nc = bp + bs
    nc_pad = -(-nc // 16) * 16
    c_all = jnp.concatenate([c_prompt, c_sample, jnp.zeros((nc_pad - nc, d), F32)], axis=0)

    hp = x_prompt.reshape(bp * sp, d)
    hs = x_sample.reshape(bs * ss, d)
    outs = [[] for _ in range(6)]
    for l in range(depth):
        mod = _mod_call(c_all, w_ada[l], b_ada[l])
        weights = _prep_weights(w_in[l], w_out[l], w_gate_up[l], w_down[l], w_gk_up[l], d)
        params = (g_norm1[l], b_gk[l], g_gla_out[l], swa_sinks[l], g_norm2[l])
        hp, s_p, k_p, v_p, weights = _layer(hp, mod[:bp], pos_p, s0_p, None, None, weights, params, bp, sp)
        hs, s_s, k_s, v_s, _ = _layer(hs, mod[bp:nc], pos_s, state_gla[l],
                                   cache_swa_k[l].reshape(bs, WINDOW, kvw),
                                   cache_swa_v[l].reshape(bs, WINDOW, kvw), weights, params, bs, ss)
        for lst, val in zip(outs, (s_p, k_p, v_p, s_s, k_s, v_s)):
            lst.append(val)
    y_prompt = _norm_call(hp, g_final, None, None, sp, F32).reshape(bp, sp, d)
    y_sample = _norm_call(hs, g_final, None, None, ss, F32).reshape(bs, ss, d)
    return (y_prompt, y_sample) + tuple(jnp.stack(o) for o in outs)
```

```python
import functools

import jax
import jax.numpy as jnp
import numpy as np
from jax import lax
from jax.experimental import pallas as pl
from jax.experimental.pallas import tpu as pltpu

F32 = jnp.float32
BF16 = jnp.bfloat16

CHUNK = 64
GLA_HEADS = 4
GLA_GATE_RANK = 16
GLA_GATE_NORM = 16.0
SWA_HEADS = 64
SWA_KV_HEADS = 8
SWA_GROUP = SWA_HEADS // SWA_KV_HEADS
WINDOW = 128
ROPE_THETA = 10000.0
NORM_EPS = 1e-6
PAST_LEN = 2048
LOG2E = 1.4426950408889634

LANES = 128
VMEM_LIMIT_BYTES = 56 * 1024 * 1024
MM_TILE_M = 1024
GLA_HEADS_PER_STEP = 4
GLA_BLOCK_CHUNKS = 4
SWA_BLOCKS_PER_STEP = 4
SWA_HEADS_PER_STAGE = 4
FF_TILE = 512


def _cparams(sem, vmem=VMEM_LIMIT_BYTES):
    return pltpu.CompilerParams(dimension_semantics=sem, vmem_limit_bytes=vmem)


def _silu(x):
    return x * (1.0 / (1.0 + jnp.exp(-x)))


def _sigmoid(x):
    return 1.0 / (1.0 + jnp.exp(-x))


def _pick(n, prefs):
    for p in prefs:
        if n % p == 0:
            return p
    return n


def _relayout_kernel(src_ref, mode_ref, main, o_ref):
    mode = mode_ref[pl.program_id(1)]

    @pl.when(mode == 0)
    def _():
        o_ref[...] = main[...].astype(BF16)

    @pl.when(mode == 2)
    def _():
        o_ref[...] = jnp.zeros(o_ref.shape, BF16)


def _relayout_call(src, width, src_blk, mode, tr=None):
    r = src.shape[0]
    tr = r if tr is None else tr
    nb = len(src_blk)
    grid_spec = pltpu.PrefetchScalarGridSpec(
        num_scalar_prefetch=2,
        grid=(r // tr, nb),
        in_specs=[pl.BlockSpec((tr, width), lambda i, b, s, m: (i, s[b]))],
        out_specs=pl.BlockSpec((tr, width), lambda i, b, s, m: (i, b)),
    )
    return pl.pallas_call(
        _relayout_kernel,
        grid_spec=grid_spec,
        out_shape=jax.ShapeDtypeStruct((r, nb * width), BF16),
        compiler_params=_cparams(("parallel", "arbitrary")),
        name="weight_relayout",
    )(jnp.asarray(np.asarray(src_blk, np.int32)), jnp.asarray(np.asarray(mode, np.int32)), src)


def _relayout_t_kernel(src_ref, nxt_ref, mode_ref, main, nxt, o_ref, *, shift):
    mode = mode_ref[pl.program_id(1)]
    w = o_ref.shape[1]

    @pl.when(mode == 0)
    def _():
        o_ref[...] = jnp.transpose(main[...]).astype(BF16)

    @pl.when(mode == 1)
    def _():
        x = jnp.concatenate([main[...], nxt[...]], axis=0)
        o_ref[...] = jnp.transpose(x[shift:shift + w, :]).astype(BF16)


def _relayout_t_call(src_t, width, src_blk, mode, shift, tk):
    r = src_t.shape[1]
    nb = len(src_blk)
    src_blk = np.asarray(src_blk, np.int32)
    nxt_blk = (src_blk + 1) * (width // shift)
    grid_spec = pltpu.PrefetchScalarGridSpec(
        num_scalar_prefetch=3,
        grid=(r // tk, nb),
        in_specs=[pl.BlockSpec((width, tk), lambda i, b, s, nx, m: (s[b], i)),
                  pl.BlockSpec((shift, tk), lambda i, b, s, nx, m: (nx[b], i))],
        out_specs=pl.BlockSpec((tk, width), lambda i, b, s, nx, m: (i, b)),
    )
    return pl.pallas_call(
        functools.partial(_relayout_t_kernel, shift=shift),
        grid_spec=grid_spec,
        out_shape=jax.ShapeDtypeStruct((r, nb * width), BF16),
        compiler_params=_cparams(("parallel", "arbitrary")),
        name="weight_relayout_t",
    )(jnp.asarray(src_blk), jnp.asarray(nxt_blk, jnp.int32), jnp.asarray(np.asarray(mode, np.int32)),
      src_t, src_t)


def _mod_kernel(c_ref, w_ref, b_ref, o_ref):
    a = _silu(c_ref[...]).astype(BF16)
    w = w_ref[...].astype(BF16)
    o_ref[...] = jnp.dot(a, w, preferred_element_type=F32) + b_ref[...]


def _mod_call(c_all, w_ada, b_ada):
    m, d = c_all.shape
    n = w_ada.shape[1]
    tn = _pick(n, (512, 256, 128))
    return pl.pallas_call(
        _mod_kernel,
        grid=(n // tn,),
        in_specs=[pl.BlockSpec((m, d), lambda j: (0, 0)),
                  pl.BlockSpec((d, tn), lambda j: (0, j)),
                  pl.BlockSpec((1, tn), lambda j: (0, j))],
        out_specs=pl.BlockSpec((m, tn), lambda j: (0, j)),
        out_shape=jax.ShapeDtypeStruct((m, n), F32),
        compiler_params=_cparams(("arbitrary",)),
        name="adaln_mod",
    )(c_all, w_ada, b_ada.reshape(1, n))


def _norm_kernel(x_ref, g_ref, *rest, modulate):
    o_ref = rest[-1]
    x = x_ref[...]
    y = x * lax.rsqrt(jnp.mean(x * x, axis=-1, keepdims=True) + NORM_EPS) * g_ref[...]
    if modulate:
        sc_ref, sh_ref = rest[0], rest[1]
        y = y * (1.0 + sc_ref[...]) + sh_ref[...]
    o_ref[...] = y.astype(o_ref.dtype)


def _norm_call(x, g, sc, sh, rows_per_group, out_dtype):
    t, d = x.shape
    tr = _pick(rows_per_group, (512, 256, 128, 64, 32, 16, 8))
    tiles = rows_per_group // tr
    modulate = sc is not None
    in_specs = [pl.BlockSpec((tr, d), lambda i: (i, 0)), pl.BlockSpec((1, d), lambda i: (0, 0))]
    args = [x, g.reshape(1, d)]
    if modulate:
        in_specs += [pl.BlockSpec((None, 1, d), lambda i: (i // tiles, 0, 0))] * 2
        args += [sc, sh]
    return pl.pallas_call(
        functools.partial(_norm_kernel, modulate=modulate),
        grid=(t // tr,),
        in_specs=in_specs,
        out_specs=pl.BlockSpec((tr, d), lambda i: (i, 0)),
        out_shape=jax.ShapeDtypeStruct((t, d), out_dtype),
        compiler_params=_cparams(("parallel",)),
        name="rmsnorm_mod",
    )(*args)


def _mm_kernel(*refs, epilogue, n_side, side_blocks, nj):
    tabs = refs[:2 * n_side]
    a_ref, w_ref = refs[2 * n_side:2 * n_side + 2]
    rest = refs[2 * n_side + 2:]
    n_extra = 2 if epilogue == "residual" else 0
    extras = rest[:n_extra]
    side_in = rest[n_extra:n_extra + n_side]
    o_ref = rest[n_extra + n_side]
    side_out = rest[n_extra + n_side + 1:]

    acc = jnp.dot(a_ref[...], w_ref[...], preferred_element_type=F32)
    if epilogue == "residual":
        res_ref, gate_ref = extras
        out = res_ref[...] + gate_ref[...] * acc
    elif epilogue == "swiglu":
        half = acc.shape[1] // 2
        out = _silu(acc[:, :half]) * acc[:, half:]
    else:
        out = acc
    o_ref[...] = out.astype(o_ref.dtype)

    t = pl.program_id(0) * nj + pl.program_id(1)
    for q in range(n_side):
        nrt, nb = side_blocks[q]

        mode = tabs[2 * q + 1][jnp.minimum(t, nrt * nb - 1) % nb]
        live = t < nrt * nb

        @pl.when(jnp.logical_and(live, mode == 0))
        def _(q=q):
            side_out[q][...] = side_in[q][...].astype(BF16)

        @pl.when(jnp.logical_and(live, mode == 2))
        def _(q=q):
            side_out[q][...] = jnp.zeros(side_out[q].shape, BF16)


def _mm_call(a, w, *, tm, tn, epilogue="plain", res=None, gate=None, rows_per_group=None,
             out_dtype=F32, name="matmul", side=(), vmem=VMEM_LIMIT_BYTES):
    m = a.shape[0]
    kdim, n = w.shape
    n_out = n // 2 if epilogue == "swiglu" else n
    tn_out = tn // 2 if epilogue == "swiglu" else tn
    ni, nj = m // tm, n // tn
    n_side = len(side)
    in_specs = [pl.BlockSpec((tm, kdim), lambda i, j, *_: (i, 0)),
                pl.BlockSpec((kdim, tn), lambda i, j, *_: (0, j))]
    args = [a, w]
    if epilogue == "residual":
        in_specs.append(pl.BlockSpec((tm, tn), lambda i, j, *_: (i, j)))
        if rows_per_group % tm == 0:
            tiles = rows_per_group // tm
            in_specs.append(pl.BlockSpec((None, 1, tn), lambda i, j, *_: (i // tiles, 0, j)))
        else:
            gate = jnp.repeat(gate[:, 0, :], rows_per_group, axis=0)
            in_specs.append(pl.BlockSpec((tm, tn), lambda i, j, *_: (i, j)))
        args += [res, gate]
    out_specs = [pl.BlockSpec((tm, tn_out), lambda i, j, *_: (i, j))]
    out_shape = [jax.ShapeDtypeStruct((m, n_out), out_dtype)]
    tabs, side_blocks = [], []
    for q, (src, tr, width, src_blk, mode) in enumerate(side):
        nrt, nb = src.shape[0] // tr, len(src_blk)
        assert nrt * nb <= ni * nj, "not enough grid steps for this side job"
        side_blocks.append((nrt, nb))
        tabs += [jnp.asarray(np.asarray(src_blk, np.int32)), jnp.asarray(np.asarray(mode, np.int32))]

        def src_map(i, j, *t, q=q, nrt=nrt, nb=nb):
            step = jnp.minimum(i * nj + j, nrt * nb - 1)
            return (step // nb, t[2 * q][step % nb])

        def dst_map(i, j, *t, nrt=nrt, nb=nb):
            step = jnp.minimum(i * nj + j, nrt * nb - 1)
            return (step // nb, step % nb)

        in_specs.append(pl.BlockSpec((tr, width), src_map))
        args.append(src)
        out_specs.append(pl.BlockSpec((tr, width), dst_map))
        out_shape.append(jax.ShapeDtypeStruct((src.shape[0], nb * width), BF16))
    grid_spec = pltpu.PrefetchScalarGridSpec(num_scalar_prefetch=2 * n_side, grid=(ni, nj),
                                            in_specs=in_specs, out_specs=out_specs)
    outs = pl.pallas_call(
        functools.partial(_mm_kernel, epilogue=epilogue, n_side=n_side, side_blocks=tuple(side_blocks), nj=nj),
        grid_spec=grid_spec,
        out_shape=out_shape,
        compiler_params=_cparams(("arbitrary", "arbitrary") if n_side else ("parallel", "parallel"), vmem),
        name=name,
    )(*tabs, *args)
    return outs if n_side else outs[0]


def _rope_tables(pos, head_dim):
    half = head_dim // 2
    inv = ROPE_THETA ** (-jnp.arange(half, dtype=F32) / half)
    ang = pos.astype(F32)[:, None] * inv[None, :]
    cos, sin = jnp.cos(ang), jnp.sin(ang)
    reps = LANES // head_dim
    cos_t = jnp.tile(jnp.concatenate([cos, cos], axis=1), (1, reps))
    sin_t = jnp.tile(jnp.concatenate([-sin, sin], axis=1), (1, reps))
    return cos_t, sin_t


def _rope_block(x, cos_t, sin_t, first_half):
    partner = jnp.where(first_half, pltpu.roll(x, 96, 1), pltpu.roll(x, 32, 1))
    return x * cos_t + partner * sin_t


def _first_half_mask(rows):
    lane = lax.broadcasted_iota(jnp.int32, (rows, LANES), 1)
    return (lane % 64) < 32


def _rope_k_kernel(k_ref, cos_ref, sin_ref, o_ref):
    rows, width = k_ref.shape
    fh = _first_half_mask(rows)
    cos_t, sin_t = cos_ref[...], sin_ref[...]
    for c in range(width // LANES):
        sl = slice(c * LANES, (c + 1) * LANES)
        o_ref[:, sl] = _rope_block(k_ref[:, sl].astype(F32), cos_t, sin_t, fh)


def _rope_k_call(proj, col_block, width, cos_t, sin_t, rows_per_group):
    t = proj.shape[0]
    tr = _pick(rows_per_group, (1024, 512, 256, 128, 64, 32, 16, 8))
    tiles = rows_per_group // tr
    return pl.pallas_call(
        _rope_k_kernel,
        grid=(t // tr,),
        in_specs=[pl.BlockSpec((tr, width), lambda i: (i, col_block)),
                  pl.BlockSpec((tr, LANES), lambda i: (i % tiles, 0)),
                  pl.BlockSpec((tr, LANES), lambda i: (i % tiles, 0))],
        out_specs=pl.BlockSpec((tr, width), lambda i: (i, 0)),
        out_shape=jax.ShapeDtypeStruct((t, width), F32),
        compiler_params=_cparams(("parallel",)),
        name="rope_k",
    )(proj, cos_t, sin_t)


def _gla_kernel(q_ref, k_ref, v_ref, ga_ref, ra_ref, wgk_ref, bgk_ref, g_ref, s0_ref,
                o_ref, sout_ref, s_scr, *, n_steps, c, scale, hp):
    n = pl.program_id(2)
    rows = q_ref.shape[0]
    dk = q_ref.shape[1] // hp
    dv = v_ref.shape[1] // hp
    nr = rows // c
    nt = (((1,), (1,)), ((), ()))
    tn = (((0,), (0,)), ((), ()))
    heads = range(hp)
    ksl = lambda u: slice(u * dk, (u + 1) * dk)
    vsl = lambda u: slice(u * dv, (u + 1) * dv)

    @pl.when(n == 0)
    def _():
        s_scr[...] = s0_ref[...]

    ra = ra_ref[...]
    lane = lax.broadcasted_iota(jnp.int32, ra.shape, 1)
    ra_hi = ra.astype(BF16)
    ra_lo = (ra - ra_hi.astype(F32)).astype(BF16)
    lhs = jnp.where(jnp.logical_and(lane >= GLA_GATE_RANK, lane < 2 * GLA_GATE_RANK), ra_lo, ra_hi)
    z = jnp.dot(lhs, wgk_ref[...], preferred_element_type=F32) + bgk_ref[...]
    log_a = (jnp.minimum(z, 0.0) - jnp.log(1.0 + jnp.exp(-jnp.abs(z)))) * (1.0 / GLA_GATE_NORM)

    ri = lax.broadcasted_iota(jnp.int32, (rows, 2 * rows), 0)
    ci = lax.broadcasted_iota(jnp.int32, (rows, 2 * rows), 1) % rows
    tril2 = jnp.where(jnp.logical_and(ri // c == ci // c, ci <= ri), 1.0, 0.0).astype(BF16)
    la_hi = log_a.astype(BF16)
    la_lo = (log_a - la_hi.astype(F32)).astype(BF16)
    b_all = jnp.dot(tril2, jnp.concatenate([la_hi, la_lo], axis=0), preferred_element_type=F32)

    row = lax.broadcasted_iota(jnp.int32, (c, c), 0)
    col = lax.broadcasted_iota(jnp.int32, (c, c), 1)
    causal = row >= col

    vs, qds, kds, kd32, o_parts = [], [], [], [], []
    cum = [None]
    for r in range(nr):
        sl = slice(r * c, (r + 1) * c)
        b = b_all[sl, :]
        b_mid = b[c // 2:c // 2 + 1, :]
        b_last = b[c - 1:c, :]
        q = q_ref[sl, :].astype(F32) * scale
        k = k_ref[sl, :].astype(F32)
        v = v_ref[sl, :]
        qs = (q * jnp.exp(b - b_mid)).astype(BF16)
        ks = (k * jnp.exp(b_mid - b)).astype(BF16)
        sc = [lax.dot_general(qs[:, ksl(u)], ks[:, ksl(u)], nt, preferred_element_type=F32) for u in heads]
        o = [jnp.dot(jnp.where(causal, sc[u], 0.0).astype(BF16), v[:, vsl(u)], preferred_element_type=F32)
             for u in heads]
        qd = q * jnp.exp(b)
        for rp in range(r):
            fac = cum[r] - cum[rp + 1] if rp + 1 < r else None
            qx = (qd if fac is None else qd * jnp.exp(fac)).astype(BF16)
            sx = [lax.dot_general(qx[:, ksl(u)], kds[rp][:, ksl(u)], nt, preferred_element_type=F32)
                  for u in heads]
            o = [o[u] + jnp.dot(sx[u].astype(BF16), vs[rp][:, vsl(u)], preferred_element_type=F32)
                 for u in heads]
        qds.append(qd if cum[r] is None else qd * jnp.exp(cum[r]))
        kd = k * jnp.exp(b_last - b)
        kd32.append(kd)
        kds.append(kd.astype(BF16))
        vs.append(v)
        o_parts.append(o)
        cum.append(b_last if cum[r] is None else cum[r] + b_last)

    q_all = jnp.concatenate(qds, axis=0).astype(BF16) if nr > 1 else qds[0].astype(BF16)
    total = cum[nr]
    k_parts = []
    for r in range(nr):
        if r + 1 < nr:
            k_parts.append((kd32[r] * jnp.exp(total - cum[r + 1])).astype(BF16))
        else:
            k_parts.append(kds[r])
    k_all = jnp.concatenate(k_parts, axis=0) if nr > 1 else k_parts[0]
    dec = jnp.exp(total)
    wblk = 2 * LANES
    dec_w = []
    for u in heads:
        dec_col = jnp.transpose(jnp.broadcast_to(dec[:, ksl(u)], (LANES, dk)))
        dec_w.append(jnp.concatenate([dec_col] * (wblk // LANES), axis=1))
    o_inter = [[] for _ in heads]
    for j in range(dv // wblk):
        for u in heads:
            sl = slice(j * wblk, (j + 1) * wblk)
            vcol = slice(u * dv + j * wblk, u * dv + (j + 1) * wblk)
            s_blk = s_scr[u, :, sl]
            o_inter[u].append(jnp.dot(q_all[:, ksl(u)], s_blk.astype(BF16), preferred_element_type=F32))
            upd = lax.dot_general(k_all[:, ksl(u)], v_ref[:, vcol], tn, preferred_element_type=F32)
            s_scr[u, :, sl] = s_blk * dec_w[u] + upd

    g = g_ref[...]
    for r in range(nr):
        sl = slice(r * c, (r + 1) * c)
        for u in heads:
            o = o_parts[r][u] + jnp.concatenate([p[sl, :] for p in o_inter[u]], axis=1)
            on = o * lax.rsqrt(jnp.mean(o * o, axis=-1, keepdims=True) + NORM_EPS) * g
            o_ref[sl, vsl(u)] = (on * _silu(ga_ref[sl, vsl(u)].astype(F32))).astype(o_ref.dtype)

    @pl.when(n == n_steps - 1)
    def _():
        sout_ref[...] = s_scr[...]


def _gla_call(proj, ra3, wgk3, b_gk, g_gla_out, s0, batch, seq):
    h = GLA_HEADS
    hp = GLA_HEADS_PER_STEP
    hg = h // hp
    dk = wgk3.shape[1] // h
    dv = g_gla_out.shape[0]
    c = min(CHUNK, seq)
    rows = c * GLA_BLOCK_CHUNKS if seq % (c * GLA_BLOCK_CHUNKS) == 0 else c
    ns = seq // rows
    t = batch * seq
    wk, wv = hp * dk, hp * dv
    k_off = (h * dk) // wk
    v_off = (2 * h * dk) // wv
    g_off = v_off + hg
    row = lambda b, hh, n: b * ns + n
    state_mode = dict(pipeline_mode=pl.Buffered(1)) if ns > 1 else {}
    kern = functools.partial(_gla_kernel, n_steps=ns, c=c, scale=dk ** -0.5, hp=hp)
    return pl.pallas_call(
        kern,
        grid=(batch, hg, ns),
        in_specs=[
            pl.BlockSpec((rows, wk), lambda b, hh, n: (row(b, hh, n), hh)),
            pl.BlockSpec((rows, wk), lambda b, hh, n: (row(b, hh, n), k_off + hh)),
            pl.BlockSpec((rows, wv), lambda b, hh, n: (row(b, hh, n), v_off + hh)),
            pl.BlockSpec((rows, wv), lambda b, hh, n: (row(b, hh, n), g_off + hh)),
            pl.BlockSpec((rows, LANES), lambda b, hh, n: (row(b, hh, n), 0)),
            pl.BlockSpec((LANES, wk), lambda b, hh, n: (0, hh)),
            pl.BlockSpec((1, wk), lambda b, hh, n: (0, hh)),
            pl.BlockSpec((1, dv), lambda b, hh, n: (0, 0)),
            pl.BlockSpec((None, hp, dk, dv), lambda b, hh, n: (b, hh, 0, 0), **state_mode),
        ],
        out_specs=[
            pl.BlockSpec((rows, wv), lambda b, hh, n: (row(b, hh, n), hh)),
            pl.BlockSpec((None, hp, dk, dv), lambda b, hh, n: (b, hh, 0, 0), **state_mode),
        ],
        out_shape=[jax.ShapeDtypeStruct((t, h * dv), BF16),
                   jax.ShapeDtypeStruct((batch, h, dk, dv), F32)],
        scratch_shapes=[pltpu.VMEM((hp, dk, dv), F32)],
        compiler_params=_cparams(("parallel", "parallel", "arbitrary")),
        name="gla_chunked",
    )(proj, proj, proj, proj, ra3, wgk3, b_gk.reshape(1, -1), g_gla_out.reshape(1, -1), s0)


def _dup_half(x, half, lane_lo):
    rolled = pltpu.roll(x, 64, 1)
    return jnp.where(lane_lo == (half == 0), x, rolled)


def _swa_kernel(*refs, tq, n_prev, nsub, masked):
    bias_ref, q_ref, cos_ref, sin_ref = refs[0:4]
    ka_refs = refs[4:4 + n_prev]
    va_refs = refs[4 + n_prev:4 + 2 * n_prev]
    kb_ref, vb_ref, oa_ref, za_ref, zb_ref, o_ref = refs[4 + 2 * n_prev:]

    hd = LANES // 2
    grows = SWA_GROUP * tq
    fh = _first_half_mask(tq)
    q_scale = hd ** -0.5 * LOG2E

    lane_q = lax.broadcasted_iota(jnp.int32, (tq, LANES), 1) < hd
    lane_a = lax.broadcasted_iota(jnp.int32, (WINDOW, LANES), 1) < hd
    ones = jnp.ones((2 * LANES, LANES), BF16)
    zpad = jnp.zeros((LANES - tq, LANES), BF16)
    lane_g = lax.broadcasted_iota(jnp.int32, (grows, LANES), 1)
    nt = (((1,), (1,)), ((), ()))

    def chunk(prev_refs, own_ref, idx, sl):
        if idx < n_prev:
            return prev_refs[idx][:, sl]
        return own_ref[(idx - n_prev) * tq:(idx - n_prev + 1) * tq, sl]

    for s in range(nsub):
        rs = slice(s * tq, (s + 1) * tq)
        cos_t, sin_t = cos_ref[rs, :], sin_ref[rs, :]
        if masked:
            valid_a = lane_g >= (WINDOW - (pl.program_id(1) * nsub + s) * tq)

        def keys_vals(h):
            pc, half = h // 2, h % 2
            sl = slice(pc * LANES, (pc + 1) * LANES)
            if n_prev == 1:
                ka, va = ka_refs[0][:, sl], va_refs[0][:, sl]
                kb, vb = kb_ref[rs, sl], vb_ref[rs, sl]
            else:
                ka = jnp.concatenate([chunk(ka_refs, kb_ref, s, sl), chunk(ka_refs, kb_ref, s + 1, sl)], axis=0)
                va = jnp.concatenate([chunk(va_refs, vb_ref, s, sl), chunk(va_refs, vb_ref, s + 1, sl)], axis=0)
                kb, vb = chunk(ka_refs, kb_ref, s + 2, sl), chunk(va_refs, vb_ref, s + 2, sl)
            k_all = jnp.concatenate([_dup_half(ka.astype(F32), half, lane_a).astype(BF16),
                                     _dup_half(kb.astype(F32), half, lane_q).astype(BF16), zpad], axis=0)
            v_all = jnp.concatenate([_dup_half(va.astype(F32), half, lane_a).astype(BF16),
                                     _dup_half(vb.astype(F32), half, lane_q).astype(BF16), zpad], axis=0)
            return k_all, v_all

        def queries(h):
            rows = []
            for g in range(SWA_GROUP):
                cb = h * (SWA_GROUP // 2) + g // 2
                qsl = slice(cb * LANES, (cb + 1) * LANES)
                qr = _rope_block(q_ref[rs, qsl].astype(F32), cos_t, sin_t, fh) * q_scale
                rows.append(jnp.where(lane_q == (g % 2 == 0), qr, 0.0).astype(BF16))
            return jnp.concatenate(rows, axis=0)

        def finish(h, out2):
            for j in range(SWA_GROUP // 2):
                cb = h * (SWA_GROUP // 2) + j
                osl = slice(cb * LANES, (cb + 1) * LANES)
                ob = jnp.where(lane_q, out2[(2 * j) * tq:(2 * j + 1) * tq, :],
                               out2[(2 * j + 1) * tq:(2 * j + 2) * tq, :])
                merged = (_sigmoid(za_ref[rs, osl].astype(F32)) * oa_ref[rs, osl].astype(F32)
                          + _sigmoid(zb_ref[rs, osl].astype(F32)) * ob)
                o_ref[rs, osl] = merged.astype(o_ref.dtype)

        hs_per = SWA_HEADS_PER_STAGE
        for h0 in range(0, SWA_KV_HEADS, hs_per):
            hs = range(h0, h0 + hs_per)
            kv = [keys_vals(h) for h in hs]
            lhs = [queries(h) for h in hs]
            logits = [lax.dot_general(lhs[i], kv[i][0], nt, preferred_element_type=F32) for i in range(hs_per)]
            l1 = [jnp.where(valid_a, lg[:, :LANES], -jnp.inf) if masked else lg[:, :LANES] for lg in logits]
            l2 = [logits[i][:, LANES:] + bias_ref[h0 + i] for i in range(hs_per)]
            m = [jnp.max(jnp.maximum(l1[i], l2[i]), axis=-1, keepdims=True) for i in range(hs_per)]
            p = [jnp.concatenate([jnp.exp2(l1[i] - m[i]), jnp.exp2(l2[i] - m[i])], axis=1).astype(BF16)
                 for i in range(hs_per)]
            pv = [jnp.dot(p[i], kv[i][1], preferred_element_type=F32) for i in range(hs_per)]
            den = [jnp.dot(p[i], ones, preferred_element_type=F32) for i in range(hs_per)]
            for i in range(hs_per):
                finish(h0 + i, pv[i] / den[i])


def _swa_call(proj, q_blk, k_rot, o_a, sinks, cos_t, sin_t, batch, seq, cache_k=None, cache_v=None):
    t, d = o_a.shape
    kvw = k_rot.shape[1]
    v_blk = ((q_blk + 3) * d) // kvw + 1
    if cache_k is None:
        tq = CHUNK
        nsub = SWA_BLOCKS_PER_STEP if seq % (SWA_BLOCKS_PER_STEP * tq) == 0 else 1
        nq = seq // (nsub * tq)
        rowi = lambda b, n: b * nq + n
        n_prev = WINDOW // CHUNK
        prev = lambda b, n, back: b * (seq // tq) + jnp.maximum(n * nsub - back, 0)
        ka_args = [k_rot] * n_prev
        va_args = [proj] * n_prev
        ka_specs = [pl.BlockSpec((tq, kvw), functools.partial(lambda b, n, back: (prev(b, n, back), 0),
                                                               back=n_prev - i)) for i in range(n_prev)]
        va_specs = [pl.BlockSpec((tq, kvw), functools.partial(lambda b, n, back: (prev(b, n, back), v_blk),
                                                               back=n_prev - i)) for i in range(n_prev)]
        masked = True
    else:
        tq = seq
        nsub = 1
        nq = 1
        rowi = lambda b, n: b
        n_prev = 1
        ka_args, va_args = [cache_k], [cache_v]
        ka_specs = [pl.BlockSpec((None, WINDOW, kvw), lambda b, n: (b, 0, 0))]
        va_specs = [pl.BlockSpec((None, WINDOW, kvw), lambda b, n: (b, 0, 0))]
        masked = False
    rows = nsub * tq
    wide = lambda blk: pl.BlockSpec((rows, d), lambda b, n: (rowi(b, n), blk))
    tab = pl.BlockSpec((rows, LANES), lambda b, n: (n, 0))
    lane = jnp.arange(LANES)
    sink_l2 = (sinks.astype(F32) * LOG2E).reshape(SWA_KV_HEADS, SWA_GROUP, 1, 1)
    bias_row = jnp.where(lane < tq, 0.0, jnp.where(lane == tq, sink_l2, -jnp.inf))
    sinks = jnp.broadcast_to(bias_row, (SWA_KV_HEADS, SWA_GROUP, tq, LANES)).reshape(
        SWA_KV_HEADS, SWA_GROUP * tq, LANES)
    in_specs = ([pl.BlockSpec(sinks.shape, lambda b, n: (0, 0, 0)), wide(q_blk), tab, tab] + ka_specs + va_specs +
                [pl.BlockSpec((rows, kvw), lambda b, n: (rowi(b, n), 0)),
                 pl.BlockSpec((rows, kvw), lambda b, n: (rowi(b, n), v_blk)),
                 wide(0), wide(q_blk + 1), wide(q_blk + 2)])
    args = [sinks, proj, cos_t, sin_t] + ka_args + va_args + [k_rot, proj, o_a, proj, proj]
    return pl.pallas_call(
        functools.partial(_swa_kernel, tq=tq, n_prev=n_prev, nsub=nsub, masked=masked),
        grid=(batch, nq),
        in_specs=in_specs,
        out_specs=pl.BlockSpec((rows, d), lambda b, n: (rowi(b, n), 0)),
        out_shape=jax.ShapeDtypeStruct((t, d), BF16),
        compiler_params=_cparams(("parallel", "arbitrary")),
        name="swa_merge",
    )(*args)


def _prep_weights(w_in, w_out, w_gate_up, w_down, w_gk_up, d):
    h = GLA_HEADS
    key = (d // (2 * h)) * h
    n1 = 2 * key + 2 * d
    kvw = SWA_KV_HEADS * (d // SWA_HEADS)
    w = kvw
    nb1 = n1 // w
    per_d = d // w
    q_s = list(range(per_d))
    k_s, v_s = per_d, per_d + 1
    za_s = [per_d + 2 + i for i in range(per_d)]
    zb_s = [2 * per_d + 2 + i for i in range(per_d)]
    shifted = q_s + za_s + zb_s + [k_s, v_s]
    src_blk = list(range(nb1)) + [nb1 + s for s in shifted]
    mode = [0] * nb1 + [1] * len(shifted)
    w_in_t = jnp.transpose(w_in)
    w12 = _relayout_t_call(w_in_t, w, src_blk, mode, shift=GLA_GATE_RANK, tk=2048)

    ra_cols = jnp.transpose(w_in_t[n1:n1 + GLA_GATE_RANK, :])
    w_r = jnp.pad(jnp.concatenate([ra_cols] * 3, axis=1),
                  ((0, 0), (0, LANES - 3 * GLA_GATE_RANK))).astype(BF16)
    wg_hi = w_gk_up.astype(BF16)
    wg_lo = (w_gk_up - wg_hi.astype(F32)).astype(BF16)
    wgk3 = jnp.pad(jnp.concatenate([wg_hi, wg_hi, wg_lo], axis=0),
                   ((0, LANES - 3 * GLA_GATE_RANK), (0, 0)))

    lanes_blk = list(range(w_out.shape[1] // LANES))
    job_o = (w_out, w_out.shape[0], LANES, lanes_blk, [0] * len(lanes_blk))

    dff = w_down.shape[0]
    n_src = dff // LANES
    n_tiles = -(-dff // FF_TILE)
    per_tile = FF_TILE // LANES
    src_blk, mode = [], []
    for j in range(n_tiles):
        for part in range(2):
            for sb in range(per_tile):
                blk = j * per_tile + sb
                ok = blk < n_src
                src_blk.append(part * n_src + blk if ok else 0)
                mode.append(0 if ok else 2)
    job_gu = (w_gate_up, w_gate_up.shape[0], LANES, src_blk, mode)
    cols_dn = list(range(d // 256))
    job_dn = (w_down, dff // 4, 256, cols_dn, [0] * len(cols_dn))
    return w12, w_r, wgk3, job_o, job_gu, job_dn


def _run_job(job):
    src, tr, width, src_blk, mode = job
    return _relayout_call(src, width, src_blk, mode, tr=tr)


def _layer(x, mod, pos, s0, cache_k, cache_v, weights, params, batch, seq):
    w12, w_r, wgk3, w_o, w_gu, w_dn = weights
    g_norm1, b_gk, g_gla_out, sinks, g_norm2 = params
    t, d = x.shape
    sh1, sc1, gt1, sh2, sc2, gt2 = [m.reshape(batch, 1, d) for m in jnp.split(mod, 6, axis=-1)]
    kvw = SWA_KV_HEADS * (d // SWA_HEADS)
    hd = d // SWA_HEADS
    tm = min(MM_TILE_M, t)
    q_blk = (w12.shape[1] - 3 * d - 2 * kvw) // d

    def n_blocks(job):
        return (job[0].shape[0] // job[1]) * len(job[3])

    h = _norm_call(x, g_norm1, sc1, sh1, seq, BF16)
    steps = (t // tm) * (w12.shape[1] // 1024)
    if isinstance(w_gu, tuple) and max(n_blocks(w_gu), n_blocks(w_o)) <= steps:
        proj, w_gu, w_o = _mm_call(h, w12, tm=tm, tn=1024, out_dtype=BF16, name="in_proj", side=(w_gu, w_o))
    else:
        proj = _mm_call(h, w12, tm=tm, tn=1024, out_dtype=BF16, name="in_proj")
        if isinstance(w_gu, tuple):
            w_gu, w_o = _run_job(w_gu), _run_job(w_o)
    ra3 = _mm_call(h, w_r, tm=tm, tn=LANES, name="in_proj_rank")

    o_a, s_new = _gla_call(proj, ra3, wgk3, b_gk, g_gla_out, s0, batch, seq)

    cos_t, sin_t = _rope_tables(pos, hd)
    k_blk = ((q_blk + 3) * d) // kvw
    k_rot = _rope_k_call(proj, k_blk, kvw, cos_t, sin_t, seq)
    merged = _swa_call(proj, q_blk, k_rot, o_a, sinks, cos_t, sin_t, batch, seq, cache_k, cache_v)

    x1 = _mm_call(merged, w_o, tm=tm, tn=1024, epilogue="residual", res=x, gate=gt1,
                  rows_per_group=seq, name="out_proj")
    h2 = _norm_call(x1, g_norm2, sc2, sh2, seq, BF16)
    steps = (t // tm) * (w_gu.shape[1] // (2 * FF_TILE))
    if isinstance(w_dn, tuple) and n_blocks(w_dn) <= steps:
        act, w_dn = _mm_call(h2, w_gu, tm=tm, tn=2 * FF_TILE, epilogue="swiglu", out_dtype=BF16,
                             name="ffn_gate_up", side=(w_dn,))
    else:
        act = _mm_call(h2, w_gu, tm=tm, tn=2 * FF_TILE, epilogue="swiglu", out_dtype=BF16, name="ffn_gate_up")
        if isinstance(w_dn, tuple):
            w_dn = _run_job(w_dn)
    x2 = _mm_call(act, w_dn, tm=min(MM_TILE_M // 2, t), tn=512, epilogue="residual", res=x1, gate=gt2,
                  rows_per_group=seq, name="ffn_down")

    keep = WINDOW if cache_k is None else seq
    k_keep = jnp.stack([k_rot[(b + 1) * seq - keep:(b + 1) * seq] for b in range(batch)]).reshape(
        batch, keep, SWA_KV_HEADS, hd)
    v_cols = slice((k_blk + 1) * kvw, (k_blk + 2) * kvw)
    v_new = jnp.stack([proj[(b + 1) * seq - keep:(b + 1) * seq, v_cols] for b in range(batch)])
    v_keep = v_new.astype(F32).reshape(batch, keep, SWA_KV_HEADS, hd)
    return x2, s_new, k_keep, v_keep, (w12, w_r, wgk3, w_o, w_gu, w_dn)


def kernel(x_prompt, x_sample, state_gla, cache_swa_k, cache_swa_v, c_prompt, c_sample, w_ada, b_ada,
           g_norm1, w_in, w_gk_up, b_gk, g_gla_out, swa_sinks, w_out, g_norm2, w_gate_up, w_down, g_final):
    bp, sp, d = x_prompt.shape
    bs, ss, _ = x_sample.shape
    depth = w_ada.shape[0]
    dk = w_gk_up.shape[2] // GLA_HEADS
    dv = g_gla_out.shape[1]
    kvw = SWA_KV_HEADS * (d // SWA_HEADS)

    pos_p = jnp.arange(sp)
    pos_s = PAST_LEN + jnp.arange(ss)
    s0_p = jnp.zeros((bp, GLA_HEADS, dk, dv), F32)

    nc = bp + bs
    nc_pad = -(-nc // 16) * 16
    c_all = jnp.concatenate([c_prompt, c_sample, jnp.zeros((nc_pad - nc, d), F32)], axis=0)

    hp = x_prompt.reshape(bp * sp, d)
    hs = x_sample.reshape(bs * ss, d)
    outs = [[] for _ in range(6)]
    for l in range(depth):
        mod = _mod_call(c_all, w_ada[l], b_ada[l])
        weights = _prep_weights(w_in[l], w_out[l], w_gate_up[l], w_down[l], w_gk_up[l], d)
        params = (g_norm1[l], b_gk[l], g_gla_out[l], swa_sinks[l], g_norm2[l])
        hp, s_p, k_p, v_p, weights = _layer(hp, mod[:bp], pos_p, s0_p, None, None, weights, params, bp, sp)
        hs, s_s, k_s, v_s, _ = _layer(hs, mod[bp:nc], pos_s, state_gla[l],
                                   cache_swa_k[l].reshape(bs, WINDOW, kvw),
                                   cache_swa_v[l].reshape(bs, WINDOW, kvw), weights, params, bs, ss)
        for lst, val in zip(outs, (s_p, k_p, v_p, s_s, k_s, v_s)):
            lst.append(val)
    y_prompt = _norm_call(hp, g_final, None, None, sp, F32).reshape(bp, sp, d)
    y_sample = _norm_call(hs, g_final, None, None, ss, F32).reshape(bs, ss, d)
    return (y_prompt, y_sample) + tuple(jnp.stack(o) for o in outs)
```

```python
import functools

import jax
import jax.numpy as jnp
import numpy as np
from jax import lax
from jax.experimental import pallas as pl
from jax.experimental.pallas import tpu as pltpu

F32 = jnp.float32
BF16 = jnp.bfloat16

CHUNK = 64
GLA_HEADS = 4
GLA_GATE_RANK = 16
GLA_GATE_NORM = 16.0
SWA_HEADS = 64
SWA_KV_HEADS = 8
SWA_GROUP = SWA_HEADS // SWA_KV_HEADS
WINDOW = 128
ROPE_THETA = 10000.0
NORM_EPS = 1e-6
PAST_LEN = 2048
LOG2E = 1.4426950408889634

LANES = 128
VMEM_LIMIT_BYTES = 56 * 1024 * 1024
MM_TILE_M = 1024
GLA_HEADS_PER_STEP = 4
GLA_BLOCK_CHUNKS = 4
SWA_BLOCKS_PER_STEP = 4
SWA_HEADS_PER_STAGE = 4
FF_TILE = 512


def _cparams(sem):
    return pltpu.CompilerParams(dimension_semantics=sem, vmem_limit_bytes=VMEM_LIMIT_BYTES)


def _silu(x):
    return x * (1.0 / (1.0 + jnp.exp(-x)))


def _sigmoid(x):
    return 1.0 / (1.0 + jnp.exp(-x))


def _pick(n, prefs):
    for p in prefs:
        if n % p == 0:
            return p
    return n


def _relayout_kernel(src_ref, mode_ref, main, o_ref):
    mode = mode_ref[pl.program_id(1)]

    @pl.when(mode == 0)
    def _():
        o_ref[...] = main[...].astype(BF16)

    @pl.when(mode == 2)
    def _():
        o_ref[...] = jnp.zeros(o_ref.shape, BF16)


def _relayout_call(src, width, src_blk, mode, tr=None):
    r = src.shape[0]
    tr = r if tr is None else tr
    nb = len(src_blk)
    grid_spec = pltpu.PrefetchScalarGridSpec(
        num_scalar_prefetch=2,
        grid=(r // tr, nb),
        in_specs=[pl.BlockSpec((tr, width), lambda i, b, s, m: (i, s[b]))],
        out_specs=pl.BlockSpec((tr, width), lambda i, b, s, m: (i, b)),
    )
    return pl.pallas_call(
        _relayout_kernel,
        grid_spec=grid_spec,
        out_shape=jax.ShapeDtypeStruct((r, nb * width), BF16),
        compiler_params=_cparams(("parallel", "arbitrary")),
        name="weight_relayout",
    )(jnp.asarray(np.asarray(src_blk, np.int32)), jnp.asarray(np.asarray(mode, np.int32)), src)


def _relayout_t_kernel(src_ref, nxt_ref, mode_ref, main, nxt, o_ref, *, shift):
    mode = mode_ref[pl.program_id(1)]
    w = o_ref.shape[1]

    @pl.when(mode == 0)
    def _():
        o_ref[...] = jnp.transpose(main[...]).astype(BF16)

    @pl.when(mode == 1)
    def _():
        x = jnp.concatenate([main[...], nxt[...]], axis=0)
        o_ref[...] = jnp.transpose(x[shift:shift + w, :]).astype(BF16)


def _relayout_t_call(src_t, width, src_blk, mode, shift, tk):
    r = src_t.shape[1]
    nb = len(src_blk)
    src_blk = np.asarray(src_blk, np.int32)
    nxt_blk = (src_blk + 1) * (width // shift)
    grid_spec = pltpu.PrefetchScalarGridSpec(
        num_scalar_prefetch=3,
        grid=(r // tk, nb),
        in_specs=[pl.BlockSpec((width, tk), lambda i, b, s, nx, m: (s[b], i)),
                  pl.BlockSpec((shift, tk), lambda i, b, s, nx, m: (nx[b], i))],
        out_specs=pl.BlockSpec((tk, width), lambda i, b, s, nx, m: (i, b)),
    )
    return pl.pallas_call(
        functools.partial(_relayout_t_kernel, shift=shift),
        grid_spec=grid_spec,
        out_shape=jax.ShapeDtypeStruct((r, nb * width), BF16),
        compiler_params=_cparams(("parallel", "arbitrary")),
        name="weight_relayout_t",
    )(jnp.asarray(src_blk), jnp.asarray(nxt_blk, jnp.int32), jnp.asarray(np.asarray(mode, np.int32)),
      src_t, src_t)


def _mod_kernel(c_ref, w_ref, b_ref, o_ref):
    a = _silu(c_ref[...]).astype(BF16)
    w = w_ref[...].astype(BF16)
    o_ref[...] = jnp.dot(a, w, preferred_element_type=F32) + b_ref[...]


def _mod_call(c_all, w_ada, b_ada):
    m, d = c_all.shape
    n = w_ada.shape[1]
    tn = _pick(n, (512, 256, 128))
    return pl.pallas_call(
        _mod_kernel,
        grid=(n // tn,),
        in_specs=[pl.BlockSpec((m, d), lambda j: (0, 0)),
                  pl.BlockSpec((d, tn), lambda j: (0, j)),
                  pl.BlockSpec((1, tn), lambda j: (0, j))],
        out_specs=pl.BlockSpec((m, tn), lambda j: (0, j)),
        out_shape=jax.ShapeDtypeStruct((m, n), F32),
        compiler_params=_cparams(("arbitrary",)),
        name="adaln_mod",
    )(c_all, w_ada, b_ada.reshape(1, n))


def _norm_kernel(x_ref, g_ref, *rest, modulate):
    o_ref = rest[-1]
    x = x_ref[...]
    y = x * lax.rsqrt(jnp.mean(x * x, axis=-1, keepdims=True) + NORM_EPS) * g_ref[...]
    if modulate:
        sc_ref, sh_ref = rest[0], rest[1]
        y = y * (1.0 + sc_ref[...]) + sh_ref[...]
    o_ref[...] = y.astype(o_ref.dtype)


def _norm_call(x, g, sc, sh, rows_per_group, out_dtype):
    t, d = x.shape
    tr = _pick(rows_per_group, (512, 256, 128, 64, 32, 16, 8))
    tiles = rows_per_group // tr
    modulate = sc is not None
    in_specs = [pl.BlockSpec((tr, d), lambda i: (i, 0)), pl.BlockSpec((1, d), lambda i: (0, 0))]
    args = [x, g.reshape(1, d)]
    if modulate:
        in_specs += [pl.BlockSpec((None, 1, d), lambda i: (i // tiles, 0, 0))] * 2
        args += [sc, sh]
    return pl.pallas_call(
        functools.partial(_norm_kernel, modulate=modulate),
        grid=(t // tr,),
        in_specs=in_specs,
        out_specs=pl.BlockSpec((tr, d), lambda i: (i, 0)),
        out_shape=jax.ShapeDtypeStruct((t, d), out_dtype),
        compiler_params=_cparams(("parallel",)),
        name="rmsnorm_mod",
    )(*args)


def _mm_kernel(*refs, epilogue, n_side, side_blocks, nj):
    tabs = refs[:2 * n_side]
    a_ref, w_ref = refs[2 * n_side:2 * n_side + 2]
    rest = refs[2 * n_side + 2:]
    n_extra = 2 if epilogue == "residual" else 0
    extras = rest[:n_extra]
    side_in = rest[n_extra:n_extra + n_side]
    o_ref = rest[n_extra + n_side]
    side_out = rest[n_extra + n_side + 1:]

    acc = jnp.dot(a_ref[...], w_ref[...], preferred_element_type=F32)
    if epilogue == "residual":
        res_ref, gate_ref = extras
        out = res_ref[...] + gate_ref[...] * acc
    elif epilogue == "swiglu":
        half = acc.shape[1] // 2
        out = _silu(acc[:, :half]) * acc[:, half:]
    else:
        out = acc
    o_ref[...] = out.astype(o_ref.dtype)

    t = pl.program_id(0) * nj + pl.program_id(1)
    for q in range(n_side):
        nrt, nb = side_blocks[q]

        mode = tabs[2 * q + 1][jnp.minimum(t, nrt * nb - 1) % nb]
        live = t < nrt * nb

        @pl.when(jnp.logical_and(live, mode == 0))
        def _(q=q):
            side_out[q][...] = side_in[q][...].astype(BF16)

        @pl.when(jnp.logical_and(live, mode == 2))
        def _(q=q):
            side_out[q][...] = jnp.zeros(side_out[q].shape, BF16)


def _mm_call(a, w, *, tm, tn, epilogue="plain", res=None, gate=None, rows_per_group=None,
             out_dtype=F32, name="matmul", side=()):
    m = a.shape[0]
    kdim, n = w.shape
    n_out = n // 2 if epilogue == "swiglu" else n
    tn_out = tn // 2 if epilogue == "swiglu" else tn
    ni, nj = m // tm, n // tn
    n_side = len(side)
    in_specs = [pl.BlockSpec((tm, kdim), lambda i, j, *_: (i, 0)),
                pl.BlockSpec((kdim, tn), lambda i, j, *_: (0, j))]
    args = [a, w]
    if epilogue == "residual":
        in_specs.append(pl.BlockSpec((tm, tn), lambda i, j, *_: (i, j)))
        if rows_per_group % tm == 0:
            tiles = rows_per_group // tm
            in_specs.append(pl.BlockSpec((None, 1, tn), lambda i, j, *_: (i // tiles, 0, j)))
        else:
            gate = jnp.repeat(gate[:, 0, :], rows_per_group, axis=0)
            in_specs.append(pl.BlockSpec((tm, tn), lambda i, j, *_: (i, j)))
        args += [res, gate]
    out_specs = [pl.BlockSpec((tm, tn_out), lambda i, j, *_: (i, j))]
    out_shape = [jax.ShapeDtypeStruct((m, n_out), out_dtype)]
    tabs, side_blocks = [], []
    for q, (src, tr, width, src_blk, mode) in enumerate(side):
        nrt, nb = src.shape[0] // tr, len(src_blk)
        assert nrt * nb <= ni * nj, "not enough grid steps for this side job"
        side_blocks.append((nrt, nb))
        tabs += [jnp.asarray(np.asarray(src_blk, np.int32)), jnp.asarray(np.asarray(mode, np.int32))]

        def src_map(i, j, *t, q=q, nrt=nrt, nb=nb):
            step = jnp.minimum(i * nj + j, nrt * nb - 1)
            return (step // nb, t[2 * q][step % nb])

        def dst_map(i, j, *t, nrt=nrt, nb=nb):
            step = jnp.minimum(i * nj + j, nrt * nb - 1)
            return (step // nb, step % nb)

        in_specs.append(pl.BlockSpec((tr, width), src_map))
        args.append(src)
        out_specs.append(pl.BlockSpec((tr, width), dst_map))
        out_shape.append(jax.ShapeDtypeStruct((src.shape[0], nb * width), BF16))
    grid_spec = pltpu.PrefetchScalarGridSpec(num_scalar_prefetch=2 * n_side, grid=(ni, nj),
                                            in_specs=in_specs, out_specs=out_specs)
    outs = pl.pallas_call(
        functools.partial(_mm_kernel, epilogue=epilogue, n_side=n_side, side_blocks=tuple(side_blocks), nj=nj),
        grid_spec=grid_spec,
        out_shape=out_shape,
        compiler_params=_cparams(("arbitrary", "arbitrary") if n_side else ("parallel", "parallel")),
        name=name,
    )(*tabs, *args)
    return outs if n_side else outs[0]


def _rope_tables(pos, head_dim):
    half = head_dim // 2
    inv = ROPE_THETA ** (-jnp.arange(half, dtype=F32) / half)
    ang = pos.astype(F32)[:, None] * inv[None, :]
    cos, sin = jnp.cos(ang), jnp.sin(ang)
    reps = LANES // head_dim
    cos_t = jnp.tile(jnp.concatenate([cos, cos], axis=1), (1, reps))
    sin_t = jnp.tile(jnp.concatenate([-sin, sin], axis=1), (1, reps))
    return cos_t, sin_t


def _rope_block(x, cos_t, sin_t, first_half):
    partner = jnp.where(first_half, pltpu.roll(x, 96, 1), pltpu.roll(x, 32, 1))
    return x * cos_t + partner * sin_t


def _first_half_mask(rows):
    lane = lax.broadcasted_iota(jnp.int32, (rows, LANES), 1)
    return (lane % 64) < 32


def _rope_k_kernel(k_ref, cos_ref, sin_ref, o_ref):
    rows, width = k_ref.shape
    fh = _first_half_mask(rows)
    cos_t, sin_t = cos_ref[...], sin_ref[...]
    for c in range(width // LANES):
        sl = slice(c * LANES, (c + 1) * LANES)
        o_ref[:, sl] = _rope_block(k_ref[:, sl].astype(F32), cos_t, sin_t, fh)


def _rope_k_call(proj, col_block, width, cos_t, sin_t, rows_per_group):
    t = proj.shape[0]
    tr = _pick(rows_per_group, (1024, 512, 256, 128, 64, 32, 16, 8))
    tiles = rows_per_group // tr
    return pl.pallas_call(
        _rope_k_kernel,
        grid=(t // tr,),
        in_specs=[pl.BlockSpec((tr, width), lambda i: (i, col_block)),
                  pl.BlockSpec((tr, LANES), lambda i: (i % tiles, 0)),
                  pl.BlockSpec((tr, LANES), lambda i: (i % tiles, 0))],
        out_specs=pl.BlockSpec((tr, width), lambda i: (i, 0)),
        out_shape=jax.ShapeDtypeStruct((t, width), F32),
        compiler_params=_cparams(("parallel",)),
        name="rope_k",
    )(proj, cos_t, sin_t)


def _gla_kernel(q_ref, k_ref, v_ref, ga_ref, ra_ref, wgk_ref, bgk_ref, g_ref, s0_ref,
                o_ref, sout_ref, s_scr, *, n_steps, c, scale, hp):
    n = pl.program_id(2)
    rows = q_ref.shape[0]
    dk = q_ref.shape[1] // hp
    dv = v_ref.shape[1] // hp
    nr = rows // c
    nt = (((1,), (1,)), ((), ()))
    tn = (((0,), (0,)), ((), ()))
    heads = range(hp)
    ksl = lambda u: slice(u * dk, (u + 1) * dk)
    vsl = lambda u: slice(u * dv, (u + 1) * dv)

    @pl.when(n == 0)
    def _():
        s_scr[...] = s0_ref[...]

    ra = ra_ref[...]
    lane = lax.broadcasted_iota(jnp.int32, ra.shape, 1)
    ra_hi = ra.astype(BF16)
    ra_lo = (ra - ra_hi.astype(F32)).astype(BF16)
    lhs = jnp.where(jnp.logical_and(lane >= GLA_GATE_RANK, lane < 2 * GLA_GATE_RANK), ra_lo, ra_hi)
    z = jnp.dot(lhs, wgk_ref[...], preferred_element_type=F32) + bgk_ref[...]
    log_a = (jnp.minimum(z, 0.0) - jnp.log(1.0 + jnp.exp(-jnp.abs(z)))) * (1.0 / GLA_GATE_NORM)

    ri = lax.broadcasted_iota(jnp.int32, (rows, 2 * rows), 0)
    ci = lax.broadcasted_iota(jnp.int32, (rows, 2 * rows), 1) % rows
    tril2 = jnp.where(jnp.logical_and(ri // c == ci // c, ci <= ri), 1.0, 0.0).astype(BF16)
    la_hi = log_a.astype(BF16)
    la_lo = (log_a - la_hi.astype(F32)).astype(BF16)
    b_all = jnp.dot(tril2, jnp.concatenate([la_hi, la_lo], axis=0), preferred_element_type=F32)

    row = lax.broadcasted_iota(jnp.int32, (c, c), 0)
    col = lax.broadcasted_iota(jnp.int32, (c, c), 1)
    causal = row >= col

    vs, qds, kds, kd32, o_parts = [], [], [], [], []
    cum = [None]
    for r in range(nr):
        sl = slice(r * c, (r + 1) * c)
        b = b_all[sl, :]
        b_mid = b[c // 2:c // 2 + 1, :]
        b_last = b[c - 1:c, :]
        q = q_ref[sl, :].astype(F32) * scale
        k = k_ref[sl, :].astype(F32)
        v = v_ref[sl, :]
        qs = (q * jnp.exp(b - b_mid)).astype(BF16)
        ks = (k * jnp.exp(b_mid - b)).astype(BF16)
        sc = [lax.dot_general(qs[:, ksl(u)], ks[:, ksl(u)], nt, preferred_element_type=F32) for u in heads]
        o = [jnp.dot(jnp.where(causal, sc[u], 0.0).astype(BF16), v[:, vsl(u)], preferred_element_type=F32)
             for u in heads]
        qd = q * jnp.exp(b)
        for rp in range(r):
            fac = cum[r] - cum[rp + 1] if rp + 1 < r else None
            qx = (qd if fac is None else qd * jnp.exp(fac)).astype(BF16)
            sx = [lax.dot_general(qx[:, ksl(u)], kds[rp][:, ksl(u)], nt, preferred_element_type=F32)
                  for u in heads]
            o = [o[u] + jnp.dot(sx[u].astype(BF16), vs[rp][:, vsl(u)], preferred_element_type=F32)
                 for u in heads]
        qds.append(qd if cum[r] is None else qd * jnp.exp(cum[r]))
        kd = k * jnp.exp(b_last - b)
        kd32.append(kd)
        kds.append(kd.astype(BF16))
        vs.append(v)
        o_parts.append(o)
        cum.append(b_last if cum[r] is None else cum[r] + b_last)

    q_all = jnp.concatenate(qds, axis=0).astype(BF16) if nr > 1 else qds[0].astype(BF16)
    total = cum[nr]
    k_parts = []
    for r in range(nr):
        if r + 1 < nr:
            k_parts.append((kd32[r] * jnp.exp(total - cum[r + 1])).astype(BF16))
        else:
            k_parts.append(kds[r])
    k_all = jnp.concatenate(k_parts, axis=0) if nr > 1 else k_parts[0]
    dec = jnp.exp(total)
    wblk = 2 * LANES
    dec_w = []
    for u in heads:
        dec_col = jnp.transpose(jnp.broadcast_to(dec[:, ksl(u)], (LANES, dk)))
        dec_w.append(jnp.concatenate([dec_col] * (wblk // LANES), axis=1))
    o_inter = [[] for _ in heads]
    for j in range(dv // wblk):
        for u in heads:
            sl = slice(j * wblk, (j + 1) * wblk)
            vcol = slice(u * dv + j * wblk, u * dv + (j + 1) * wblk)
            s_blk = s_scr[u, :, sl]
            o_inter[u].append(jnp.dot(q_all[:, ksl(u)], s_blk.astype(BF16), preferred_element_type=F32))
            upd = lax.dot_general(k_all[:, ksl(u)], v_ref[:, vcol], tn, preferred_element_type=F32)
            s_scr[u, :, sl] = s_blk * dec_w[u] + upd

    g = g_ref[...]
    for r in range(nr):
        sl = slice(r * c, (r + 1) * c)
        for u in heads:
            o = o_parts[r][u] + jnp.concatenate([p[sl, :] for p in o_inter[u]], axis=1)
            on = o * lax.rsqrt(jnp.mean(o * o, axis=-1, keepdims=True) + NORM_EPS) * g
            o_ref[sl, vsl(u)] = (on * _silu(ga_ref[sl, vsl(u)].astype(F32))).astype(o_ref.dtype)

    @pl.when(n == n_steps - 1)
    def _():
        sout_ref[...] = s_scr[...]


def _gla_call(proj, ra3, wgk3, b_gk, g_gla_out, s0, batch, seq):
    h = GLA_HEADS
    hp = GLA_HEADS_PER_STEP
    hg = h // hp
    dk = wgk3.shape[1] // h
    dv = g_gla_out.shape[0]
    c = min(CHUNK, seq)
    rows = c * GLA_BLOCK_CHUNKS if seq % (c * GLA_BLOCK_CHUNKS) == 0 else c
    ns = seq // rows
    t = batch * seq
    wk, wv = hp * dk, hp * dv
    k_off = (h * dk) // wk
    v_off = (2 * h * dk) // wv
    g_off = v_off + hg
    row = lambda b, hh, n: b * ns + n
    state_mode = dict(pipeline_mode=pl.Buffered(1)) if ns > 1 else {}
    kern = functools.partial(_gla_kernel, n_steps=ns, c=c, scale=dk ** -0.5, hp=hp)
    return pl.pallas_call(
        kern,
        grid=(batch, hg, ns),
        in_specs=[
            pl.BlockSpec((rows, wk), lambda b, hh, n: (row(b, hh, n), hh)),
            pl.BlockSpec((rows, wk), lambda b, hh, n: (row(b, hh, n), k_off + hh)),
            pl.BlockSpec((rows, wv), lambda b, hh, n: (row(b, hh, n), v_off + hh)),
            pl.BlockSpec((rows, wv), lambda b, hh, n: (row(b, hh, n), g_off + hh)),
            pl.BlockSpec((rows, LANES), lambda b, hh, n: (row(b, hh, n), 0)),
            pl.BlockSpec((LANES, wk), lambda b, hh, n: (0, hh)),
            pl.BlockSpec((1, wk), lambda b, hh, n: (0, hh)),
            pl.BlockSpec((1, dv), lambda b, hh, n: (0, 0)),
            pl.BlockSpec((None, hp, dk, dv), lambda b, hh, n: (b, hh, 0, 0), **state_mode),
        ],
        out_specs=[
            pl.BlockSpec((rows, wv), lambda b, hh, n: (row(b, hh, n), hh)),
            pl.BlockSpec((None, hp, dk, dv), lambda b, hh, n: (b, hh, 0, 0), **state_mode),
        ],
        out_shape=[jax.ShapeDtypeStruct((t, h * dv), BF16),
                   jax.ShapeDtypeStruct((batch, h, dk, dv), F32)],
        scratch_shapes=[pltpu.VMEM((hp, dk, dv), F32)],
        compiler_params=_cparams(("parallel", "parallel", "arbitrary")),
        name="gla_chunked",
    )(proj, proj, proj, proj, ra3, wgk3, b_gk.reshape(1, -1), g_gla_out.reshape(1, -1), s0)


def _dup_half(x, half, lane_lo):
    rolled = pltpu.roll(x, 64, 1)
    return jnp.where(lane_lo == (half == 0), x, rolled)


def _swa_kernel(*refs, tq, n_prev, nsub, masked):
    bias_ref, q_ref, cos_ref, sin_ref = refs[0:4]
    ka_refs = refs[4:4 + n_prev]
    va_refs = refs[4 + n_prev:4 + 2 * n_prev]
    kb_ref, vb_ref, oa_ref, za_ref, zb_ref, o_ref = refs[4 + 2 * n_prev:]

    hd = LANES // 2
    grows = SWA_GROUP * tq
    fh = _first_half_mask(tq)

    lane_q = lax.broadcasted_iota(jnp.int32, (tq, LANES), 1) < hd
    lane_a = lax.broadcasted_iota(jnp.int32, (WINDOW, LANES), 1) < hd
    ones = jnp.ones((2 * LANES, LANES), BF16)
    zpad = jnp.zeros((LANES - tq, LANES), BF16)
    lane_g = lax.broadcasted_iota(jnp.int32, (grows, LANES), 1)
    nt = (((1,), (1,)), ((), ()))

    def chunk(prev_refs, own_ref, idx, sl):
        if idx < n_prev:
            return prev_refs[idx][:, sl]
        return own_ref[(idx - n_prev) * tq:(idx - n_prev + 1) * tq, sl]

    for s in range(nsub):
        rs = slice(s * tq, (s + 1) * tq)
        cos_t, sin_t = cos_ref[rs, :], sin_ref[rs, :]
        if masked:
            valid_a = lane_g >= (WINDOW - (pl.program_id(1) * nsub + s) * tq)

        def keys_vals(h):
            pc, half = h // 2, h % 2
            sl = slice(pc * LANES, (pc + 1) * LANES)
            if n_prev == 1:
                ka, va = ka_refs[0][:, sl], va_refs[0][:, sl]
                kb, vb = kb_ref[rs, sl], vb_ref[rs, sl]
            else:
                ka = jnp.concatenate([chunk(ka_refs, kb_ref, s, sl), chunk(ka_refs, kb_ref, s + 1, sl)], axis=0)
                va = jnp.concatenate([chunk(va_refs, vb_ref, s, sl), chunk(va_refs, vb_ref, s + 1, sl)], axis=0)
                kb, vb = chunk(ka_refs, kb_ref, s + 2, sl), chunk(va_refs, vb_ref, s + 2, sl)
            k_all = jnp.concatenate([_dup_half(ka.astype(F32), half, lane_a).astype(BF16),
                                     _dup_half(kb.astype(F32), half, lane_q).astype(BF16), zpad], axis=0)
            v_all = jnp.concatenate([_dup_half(va.astype(F32), half, lane_a).astype(BF16),
                                     _dup_half(vb.astype(F32), half, lane_q).astype(BF16), zpad], axis=0)
            return k_all, v_all

        def queries(h):
            rows = []
            for g in range(SWA_GROUP):
                cb = h * (SWA_GROUP // 2) + g // 2
                qsl = slice(cb * LANES, (cb + 1) * LANES)
                qr = _rope_block(q_ref[rs, qsl].astype(F32), cos_t, sin_t, fh)
                rows.append(jnp.where(lane_q == (g % 2 == 0), qr, 0.0).astype(BF16))
            return jnp.concatenate(rows, axis=0)

        def finish(h, out2):
            for j in range(SWA_GROUP // 2):
                cb = h * (SWA_GROUP // 2) + j
                osl = slice(cb * LANES, (cb + 1) * LANES)
                ob = jnp.where(lane_q, out2[(2 * j) * tq:(2 * j + 1) * tq, :],
                               out2[(2 * j + 1) * tq:(2 * j + 2) * tq, :])
                merged = (_sigmoid(za_ref[rs, osl].astype(F32)) * oa_ref[rs, osl].astype(F32)
                          + _sigmoid(zb_ref[rs, osl].astype(F32)) * ob)
                o_ref[rs, osl] = merged.astype(o_ref.dtype)

        hs_per = SWA_HEADS_PER_STAGE
        for h0 in range(0, SWA_KV_HEADS, hs_per):
            hs = range(h0, h0 + hs_per)
            kv = [keys_vals(h) for h in hs]
            lhs = [queries(h) for h in hs]
            logits = [lax.dot_general(lhs[i], kv[i][0], nt, preferred_element_type=F32) for i in range(hs_per)]
            l1 = [jnp.where(valid_a, lg[:, :LANES], -jnp.inf) if masked else lg[:, :LANES] for lg in logits]
            l2 = [logits[i][:, LANES:] + bias_ref[h0 + i] for i in range(hs_per)]
            m = [jnp.max(jnp.maximum(l1[i], l2[i]), axis=-1, keepdims=True) for i in range(hs_per)]
            p = [jnp.concatenate([jnp.exp2(l1[i] - m[i]), jnp.exp2(l2[i] - m[i])], axis=1).astype(BF16)
                 for i in range(hs_per)]
            pvd = [jnp.dot(p[i], jnp.concatenate([kv[i][1], ones], axis=1), preferred_element_type=F32)
                   for i in range(hs_per)]
            for i in range(hs_per):
                finish(h0 + i, pvd[i][:, :LANES] / pvd[i][:, LANES:])


def _swa_call(proj, q_blk, k_rot, o_a, sinks, cos_t, sin_t, batch, seq, cache_k=None, cache_v=None):
    t, d = o_a.shape
    kvw = k_rot.shape[1]
    v_blk = ((q_blk + 3) * d) // kvw + 1
    if cache_k is None:
        tq = CHUNK
        nsub = SWA_BLOCKS_PER_STEP if seq % (SWA_BLOCKS_PER_STEP * tq) == 0 else 1
        nq = seq // (nsub * tq)
        rowi = lambda b, n: b * nq + n
        n_prev = WINDOW // CHUNK
        prev = lambda b, n, back: b * (seq // tq) + jnp.maximum(n * nsub - back, 0)
        ka_args = [k_rot] * n_prev
        va_args = [proj] * n_prev
        ka_specs = [pl.BlockSpec((tq, kvw), functools.partial(lambda b, n, back: (prev(b, n, back), 0),
                                                               back=n_prev - i)) for i in range(n_prev)]
        va_specs = [pl.BlockSpec((tq, kvw), functools.partial(lambda b, n, back: (prev(b, n, back), v_blk),
                                                               back=n_prev - i)) for i in range(n_prev)]
        masked = True
    else:
        tq = seq
        nsub = 1
        nq = 1
        rowi = lambda b, n: b
        n_prev = 1
        ka_args, va_args = [cache_k], [cache_v]
        ka_specs = [pl.BlockSpec((None, WINDOW, kvw), lambda b, n: (b, 0, 0))]
        va_specs = [pl.BlockSpec((None, WINDOW, kvw), lambda b, n: (b, 0, 0))]
        masked = False
    rows = nsub * tq
    wide = lambda blk: pl.BlockSpec((rows, d), lambda b, n: (rowi(b, n), blk))
    tab = pl.BlockSpec((rows, LANES), lambda b, n: (n, 0))
    lane = jnp.arange(LANES)
    sink_l2 = (sinks.astype(F32) * LOG2E).reshape(SWA_KV_HEADS, SWA_GROUP, 1, 1)
    bias_row = jnp.where(lane < tq, 0.0, jnp.where(lane == tq, sink_l2, -jnp.inf))
    sinks = jnp.broadcast_to(bias_row, (SWA_KV_HEADS, SWA_GROUP, tq, LANES)).reshape(
        SWA_KV_HEADS, SWA_GROUP * tq, LANES)
    q_scale = (d // SWA_HEADS) ** -0.5 * LOG2E
    cos_q, sin_q = cos_t * q_scale, sin_t * q_scale
    in_specs = ([pl.BlockSpec(sinks.shape, lambda b, n: (0, 0, 0)), wide(q_blk), tab, tab] + ka_specs + va_specs +
                [pl.BlockSpec((rows, kvw), lambda b, n: (rowi(b, n), 0)),
                 pl.BlockSpec((rows, kvw), lambda b, n: (rowi(b, n), v_blk)),
                 wide(0), wide(q_blk + 1), wide(q_blk + 2)])
    args = [sinks, proj, cos_q, sin_q] + ka_args + va_args + [k_rot, proj, o_a, proj, proj]
    return pl.pallas_call(
        functools.partial(_swa_kernel, tq=tq, n_prev=n_prev, nsub=nsub, masked=masked),
        grid=(batch, nq),
        in_specs=in_specs,
        out_specs=pl.BlockSpec((rows, d), lambda b, n: (rowi(b, n), 0)),
        out_shape=jax.ShapeDtypeStruct((t, d), BF16),
        compiler_params=_cparams(("parallel", "arbitrary")),
        name="swa_merge",
    )(*args)


def _prep_weights(w_in, w_out, w_gate_up, w_down, w_gk_up, d):
    h = GLA_HEADS
    key = (d // (2 * h)) * h
    n1 = 2 * key + 2 * d
    kvw = SWA_KV_HEADS * (d // SWA_HEADS)
    w = kvw
    nb1 = n1 // w
    per_d = d // w
    q_s = list(range(per_d))
    k_s, v_s = per_d, per_d + 1
    za_s = [per_d + 2 + i for i in range(per_d)]
    zb_s = [2 * per_d + 2 + i for i in range(per_d)]
    shifted = q_s + za_s + zb_s + [k_s, v_s]
    src_blk = list(range(nb1)) + [nb1 + s for s in shifted]
    mode = [0] * nb1 + [1] * len(shifted)
    w_in_t = jnp.transpose(w_in)
    w12 = _relayout_t_call(w_in_t, w, src_blk, mode, shift=GLA_GATE_RANK, tk=2048)

    ra_cols = jnp.transpose(w_in_t[n1:n1 + GLA_GATE_RANK, :])
    w_r = jnp.pad(jnp.concatenate([ra_cols] * 3, axis=1),
                  ((0, 0), (0, LANES - 3 * GLA_GATE_RANK))).astype(BF16)
    wg_hi = w_gk_up.astype(BF16)
    wg_lo = (w_gk_up - wg_hi.astype(F32)).astype(BF16)
    wgk3 = jnp.pad(jnp.concatenate([wg_hi, wg_hi, wg_lo], axis=0),
                   ((0, LANES - 3 * GLA_GATE_RANK), (0, 0)))

    lanes_blk = list(range(w_out.shape[1] // LANES))
    job_o = (w_out, w_out.shape[0], LANES, lanes_blk, [0] * len(lanes_blk))

    dff = w_down.shape[0]
    n_src = dff // LANES
    n_tiles = -(-dff // FF_TILE)
    per_tile = FF_TILE // LANES
    src_blk, mode = [], []
    for j in range(n_tiles):
        for part in range(2):
            for sb in range(per_tile):
                blk = j * per_tile + sb
                ok = blk < n_src
                src_blk.append(part * n_src + blk if ok else 0)
                mode.append(0 if ok else 2)
    job_gu = (w_gate_up, w_gate_up.shape[0], LANES, src_blk, mode)
    cols_dn = list(range(d // 256))
    job_dn = (w_down, dff // 4, 256, cols_dn, [0] * len(cols_dn))
    return w12, w_r, wgk3, job_o, job_gu, job_dn


def _run_job(job):
    src, tr, width, src_blk, mode = job
    return _relayout_call(src, width, src_blk, mode, tr=tr)


def _layer(x, mod, pos, s0, cache_k, cache_v, weights, params, batch, seq):
    w12, w_r, wgk3, w_o, w_gu, w_dn = weights
    g_norm1, b_gk, g_gla_out, sinks, g_norm2 = params
    t, d = x.shape
    sh1, sc1, gt1, sh2, sc2, gt2 = [m.reshape(batch, 1, d) for m in jnp.split(mod, 6, axis=-1)]
    kvw = SWA_KV_HEADS * (d // SWA_HEADS)
    hd = d // SWA_HEADS
    tm = min(MM_TILE_M, t)
    q_blk = (w12.shape[1] - 3 * d - 2 * kvw) // d

    def n_blocks(job):
        return (job[0].shape[0] // job[1]) * len(job[3])

    h = _norm_call(x, g_norm1, sc1, sh1, seq, BF16)
    steps = (t // tm) * (w12.shape[1] // 1024)
    if isinstance(w_gu, tuple) and max(n_blocks(w_gu), n_blocks(w_o)) <= steps:
        proj, w_gu, w_o = _mm_call(h, w12, tm=tm, tn=1024, out_dtype=BF16, name="in_proj", side=(w_gu, w_o))
    else:
        proj = _mm_call(h, w12, tm=tm, tn=1024, out_dtype=BF16, name="in_proj")
        if isinstance(w_gu, tuple):
            w_gu, w_o = _run_job(w_gu), _run_job(w_o)
    ra3 = _mm_call(h, w_r, tm=tm, tn=LANES, name="in_proj_rank")

    o_a, s_new = _gla_call(proj, ra3, wgk3, b_gk, g_gla_out, s0, batch, seq)

    cos_t, sin_t = _rope_tables(pos, hd)
    k_blk = ((q_blk + 3) * d) // kvw
    k_rot = _rope_k_call(proj, k_blk, kvw, cos_t, sin_t, seq)
    merged = _swa_call(proj, q_blk, k_rot, o_a, sinks, cos_t, sin_t, batch, seq, cache_k, cache_v)

    x1 = _mm_call(merged, w_o, tm=tm, tn=1024, epilogue="residual", res=x, gate=gt1,
                  rows_per_group=seq, name="out_proj")
    h2 = _norm_call(x1, g_norm2, sc2, sh2, seq, BF16)
    steps = (t // tm) * (w_gu.shape[1] // (2 * FF_TILE))
    if isinstance(w_dn, tuple) and n_blocks(w_dn) <= steps:
        act, w_dn = _mm_call(h2, w_gu, tm=tm, tn=2 * FF_TILE, epilogue="swiglu", out_dtype=BF16,
                             name="ffn_gate_up", side=(w_dn,))
    else:
        act = _mm_call(h2, w_gu, tm=tm, tn=2 * FF_TILE, epilogue="swiglu", out_dtype=BF16, name="ffn_gate_up")
        if isinstance(w_dn, tuple):
            w_dn = _run_job(w_dn)
    x2 = _mm_call(act, w_dn, tm=min(MM_TILE_M // 2, t), tn=512, epilogue="residual", res=x1, gate=gt2,
                  rows_per_group=seq, name="ffn_down")

    keep = WINDOW if cache_k is None else seq
    k_keep = jnp.stack([k_rot[(b + 1) * seq - keep:(b + 1) * seq] for b in range(batch)]).reshape(
        batch, keep, SWA_KV_HEADS, hd)
    v_cols = slice((k_blk + 1) * kvw, (k_blk + 2) * kvw)
    v_new = jnp.stack([proj[(b + 1) * seq - keep:(b + 1) * seq, v_cols] for b in range(batch)])
    v_keep = v_new.astype(F32).reshape(batch, keep, SWA_KV_HEADS, hd)
    return x2, s_new, k_keep, v_keep, (w12, w_r, wgk3, w_o, w_gu, w_dn)


def kernel(x_prompt, x_sample, state_gla, cache_swa_k, cache_swa_v, c_prompt, c_sample, w_ada, b_ada,
           g_norm1, w_in, w_gk_up, b_gk, g_gla_out, swa_sinks, w_out, g_norm2, w_gate_up, w_down, g_final):
    bp, sp, d = x_prompt.shape
    bs, ss, _ = x_sample.shape
    depth = w_ada.shape[0]
    dk = w_gk_up.shape[2] // GLA_HEADS
    dv = g_gla_out.shape[1]
    kvw = SWA_KV_HEADS * (d // SWA_HEADS)

    pos_p = jnp.arange(sp)
    pos_s = PAST_LEN + jnp.arange(ss)
    s0_p = jnp.zeros((bp, GLA_HEADS, dk, dv), F32)

    nc = bp + bs
    nc_pad = -(-nc // 16) * 16
    c_all = jnp.concatenate([c_prompt, c_sample, jnp.zeros((nc_pad - nc, d), F32)], axis=0)

    hp = x_prompt.reshape(bp * sp, d)
    hs = x_sample.reshape(bs * ss, d)
    outs = [[] for _ in range(6)]
    for l in range(depth):
        mod = _mod_call(c_all, w_ada[l], b_ada[l])
        weights = _prep_weights(w_in[l], w_out[l], w_gate_up[l], w_down[l], w_gk_up[l], d)
        params = (g_norm1[l], b_gk[l], g_gla_out[l], swa_sinks[l], g_norm2[l])
        hp, s_p, k_p, v_p, weights = _layer(hp, mod[:bp], pos_p, s0_p, None, None, weights, params, bp, sp)
        hs, s_s, k_s, v_s, _ = _layer(hs, mod[bp:nc], pos_s, state_gla[l],
                                   cache_swa_k[l].reshape(bs, WINDOW, kvw),
                                   cache_swa_v[l].reshape(bs, WINDOW, kvw), weights, params, bs, ss)
        for lst, val in zip(outs, (s_p, k_p, v_p, s_s, k_s, v_s)):
            lst.append(val)
    y_prompt = _norm_call(hp, g_final, None, None, sp, F32).reshape(bp, sp, d)
    y_sample = _norm_call(hs, g_final, None, None, ss, F32).reshape(bs, ss, d)
    return (y_prompt, y_sample) + tuple(jnp.stack(o) for o in outs)
```

```python
import functools

import jax
import jax.numpy as jnp
import numpy as np
from jax import lax
from jax.experimental import pallas as pl
from jax.experimental.pallas import tpu as pltpu

F32 = jnp.float32
BF16 = jnp.bfloat16

CHUNK = 64
GLA_HEADS = 4
GLA_GATE_RANK = 16
GLA_GATE_NORM = 16.0
SWA_HEADS = 64
SWA_KV_HEADS = 8
SWA_GROUP = SWA_HEADS // SWA_KV_HEADS
WINDOW = 128
ROPE_THETA = 10000.0
NORM_EPS = 1e-6
PAST_LEN = 2048
LOG2E = 1.4426950408889634

LANES = 128
VMEM_LIMIT_BYTES = 56 * 1024 * 1024
VMEM_LIMIT_BYTES_RING = 62 * 1024 * 1024
WEIGHT_RING = 3
MM_TILE_M = 1024
GLA_HEADS_PER_STEP = 4
GLA_BLOCK_CHUNKS = 4
SWA_BLOCKS_PER_STEP = 4
SWA_HEADS_PER_STAGE = 4
FF_TILE = 512


def _cparams(sem):
    return pltpu.CompilerParams(dimension_semantics=sem, vmem_limit_bytes=VMEM_LIMIT_BYTES)


def _silu(x):
    return x * (1.0 / (1.0 + jnp.exp(-x)))


def _sigmoid(x):
    return 1.0 / (1.0 + jnp.exp(-x))


def _pick(n, prefs):
    for p in prefs:
        if n % p == 0:
            return p
    return n


def _relayout_kernel(src_ref, mode_ref, main, o_ref):
    mode = mode_ref[pl.program_id(1)]

    @pl.when(mode == 0)
    def _():
        o_ref[...] = main[...].astype(BF16)

    @pl.when(mode == 2)
    def _():
        o_ref[...] = jnp.zeros(o_ref.shape, BF16)


def _relayout_call(src, width, src_blk, mode, tr=None):
    r = src.shape[0]
    tr = r if tr is None else tr
    nb = len(src_blk)
    grid_spec = pltpu.PrefetchScalarGridSpec(
        num_scalar_prefetch=2,
        grid=(r // tr, nb),
        in_specs=[pl.BlockSpec((tr, width), lambda i, b, s, m: (i, s[b]))],
        out_specs=pl.BlockSpec((tr, width), lambda i, b, s, m: (i, b)),
    )
    return pl.pallas_call(
        _relayout_kernel,
        grid_spec=grid_spec,
        out_shape=jax.ShapeDtypeStruct((r, nb * width), BF16),
        compiler_params=_cparams(("parallel", "arbitrary")),
        name="weight_relayout",
    )(jnp.asarray(np.asarray(src_blk, np.int32)), jnp.asarray(np.asarray(mode, np.int32)), src)


def _relayout_t_kernel(src_ref, nxt_ref, mode_ref, main, nxt, o_ref, *, shift):
    mode = mode_ref[pl.program_id(1)]
    w = o_ref.shape[1]

    @pl.when(mode == 0)
    def _():
        o_ref[...] = jnp.transpose(main[...]).astype(BF16)

    @pl.when(mode == 1)
    def _():
        x = jnp.concatenate([main[...], nxt[...]], axis=0)
        o_ref[...] = jnp.transpose(x[shift:shift + w, :]).astype(BF16)


def _relayout_t_call(src_t, width, src_blk, mode, shift, tk):
    r = src_t.shape[1]
    nb = len(src_blk)
    src_blk = np.asarray(src_blk, np.int32)
    nxt_blk = (src_blk + 1) * (width // shift)
    grid_spec = pltpu.PrefetchScalarGridSpec(
        num_scalar_prefetch=3,
        grid=(r // tk, nb),
        in_specs=[pl.BlockSpec((width, tk), lambda i, b, s, nx, m: (s[b], i)),
                  pl.BlockSpec((shift, tk), lambda i, b, s, nx, m: (nx[b], i))],
        out_specs=pl.BlockSpec((tk, width), lambda i, b, s, nx, m: (i, b)),
    )
    return pl.pallas_call(
        functools.partial(_relayout_t_kernel, shift=shift),
        grid_spec=grid_spec,
        out_shape=jax.ShapeDtypeStruct((r, nb * width), BF16),
        compiler_params=_cparams(("parallel", "arbitrary")),
        name="weight_relayout_t",
    )(jnp.asarray(src_blk), jnp.asarray(nxt_blk, jnp.int32), jnp.asarray(np.asarray(mode, np.int32)),
      src_t, src_t)


def _mod_kernel(c_ref, w_ref, b_ref, o_ref):
    a = _silu(c_ref[...]).astype(BF16)
    w = w_ref[...].astype(BF16)
    o_ref[...] = jnp.dot(a, w, preferred_element_type=F32) + b_ref[...]


def _mod_call(c_all, w_ada, b_ada):
    m, d = c_all.shape
    n = w_ada.shape[1]
    tn = _pick(n, (512, 256, 128))
    return pl.pallas_call(
        _mod_kernel,
        grid=(n // tn,),
        in_specs=[pl.BlockSpec((m, d), lambda j: (0, 0)),
                  pl.BlockSpec((d, tn), lambda j: (0, j)),
                  pl.BlockSpec((1, tn), lambda j: (0, j))],
        out_specs=pl.BlockSpec((m, tn), lambda j: (0, j)),
        out_shape=jax.ShapeDtypeStruct((m, n), F32),
        compiler_params=_cparams(("arbitrary",)),
        name="adaln_mod",
    )(c_all, w_ada, b_ada.reshape(1, n))


def _norm_kernel(x_ref, g_ref, *rest, modulate):
    o_ref = rest[-1]
    x = x_ref[...]
    y = x * lax.rsqrt(jnp.mean(x * x, axis=-1, keepdims=True) + NORM_EPS) * g_ref[...]
    if modulate:
        sc_ref, sh_ref = rest[0], rest[1]
        y = y * (1.0 + sc_ref[...]) + sh_ref[...]
    o_ref[...] = y.astype(o_ref.dtype)


def _norm_call(x, g, sc, sh, rows_per_group, out_dtype):
    t, d = x.shape
    tr = _pick(rows_per_group, (512, 256, 128, 64, 32, 16, 8))
    tiles = rows_per_group // tr
    modulate = sc is not None
    in_specs = [pl.BlockSpec((tr, d), lambda i: (i, 0)), pl.BlockSpec((1, d), lambda i: (0, 0))]
    args = [x, g.reshape(1, d)]
    if modulate:
        in_specs += [pl.BlockSpec((None, 1, d), lambda i: (i // tiles, 0, 0))] * 2
        args += [sc, sh]
    return pl.pallas_call(
        functools.partial(_norm_kernel, modulate=modulate),
        grid=(t // tr,),
        in_specs=in_specs,
        out_specs=pl.BlockSpec((tr, d), lambda i: (i, 0)),
        out_shape=jax.ShapeDtypeStruct((t, d), out_dtype),
        compiler_params=_cparams(("parallel",)),
        name="rmsnorm_mod",
    )(*args)


def _mm_kernel(*refs, epilogue, n_side, side_blocks, nj):
    tabs = refs[:2 * n_side]
    a_ref, w_ref = refs[2 * n_side:2 * n_side + 2]
    rest = refs[2 * n_side + 2:]
    n_extra = 2 if epilogue == "residual" else 0
    extras = rest[:n_extra]
    side_in = rest[n_extra:n_extra + n_side]
    o_ref = rest[n_extra + n_side]
    side_out = rest[n_extra + n_side + 1:]

    acc = jnp.dot(a_ref[...], w_ref[...], preferred_element_type=F32)
    if epilogue == "residual":
        res_ref, gate_ref = extras
        out = res_ref[...] + gate_ref[...] * acc
    elif epilogue == "swiglu":
        half = acc.shape[1] // 2
        out = _silu(acc[:, :half]) * acc[:, half:]
    else:
        out = acc
    o_ref[...] = out.astype(o_ref.dtype)

    t = pl.program_id(0) * nj + pl.program_id(1)
    for q in range(n_side):
        nrt, nb = side_blocks[q]

        mode = tabs[2 * q + 1][jnp.minimum(t, nrt * nb - 1) % nb]
        live = t < nrt * nb

        @pl.when(jnp.logical_and(live, mode == 0))
        def _(q=q):
            side_out[q][...] = side_in[q][...].astype(BF16)

        @pl.when(jnp.logical_and(live, mode == 2))
        def _(q=q):
            side_out[q][...] = jnp.zeros(side_out[q].shape, BF16)


def _mm_call(a, w, *, tm, tn, epilogue="plain", res=None, gate=None, rows_per_group=None,
             out_dtype=F32, name="matmul", side=()):
    m = a.shape[0]
    kdim, n = w.shape
    n_out = n // 2 if epilogue == "swiglu" else n
    tn_out = tn // 2 if epilogue == "swiglu" else tn
    ni, nj = m // tm, n // tn
    n_side = len(side)
    in_specs = [pl.BlockSpec((tm, kdim), lambda i, j, *_: (i, 0)),
                pl.BlockSpec((kdim, tn), lambda i, j, *_: (0, j))]
    args = [a, w]
    if epilogue == "residual":
        in_specs.append(pl.BlockSpec((tm, tn), lambda i, j, *_: (i, j)))
        if rows_per_group % tm == 0:
            tiles = rows_per_group // tm
            in_specs.append(pl.BlockSpec((None, 1, tn), lambda i, j, *_: (i // tiles, 0, j)))
        else:
            gate = jnp.repeat(gate[:, 0, :], rows_per_group, axis=0)
            in_specs.append(pl.BlockSpec((tm, tn), lambda i, j, *_: (i, j)))
        args += [res, gate]
    out_specs = [pl.BlockSpec((tm, tn_out), lambda i, j, *_: (i, j))]
    out_shape = [jax.ShapeDtypeStruct((m, n_out), out_dtype)]
    tabs, side_blocks = [], []
    for q, (src, tr, width, src_blk, mode) in enumerate(side):
        nrt, nb = src.shape[0] // tr, len(src_blk)
        assert nrt * nb <= ni * nj, "not enough grid steps for this side job"
        side_blocks.append((nrt, nb))
        tabs += [jnp.asarray(np.asarray(src_blk, np.int32)), jnp.asarray(np.asarray(mode, np.int32))]

        def src_map(i, j, *t, q=q, nrt=nrt, nb=nb):
            step = jnp.minimum(i * nj + j, nrt * nb - 1)
            return (step // nb, t[2 * q][step % nb])

        def dst_map(i, j, *t, nrt=nrt, nb=nb):
            step = jnp.minimum(i * nj + j, nrt * nb - 1)
            return (step // nb, step % nb)

        in_specs.append(pl.BlockSpec((tr, width), src_map))
        args.append(src)
        out_specs.append(pl.BlockSpec((tr, width), dst_map))
        out_shape.append(jax.ShapeDtypeStruct((src.shape[0], nb * width), BF16))
    grid_spec = pltpu.PrefetchScalarGridSpec(num_scalar_prefetch=2 * n_side, grid=(ni, nj),
                                            in_specs=in_specs, out_specs=out_specs)
    outs = pl.pallas_call(
        functools.partial(_mm_kernel, epilogue=epilogue, n_side=n_side, side_blocks=tuple(side_blocks), nj=nj),
        grid_spec=grid_spec,
        out_shape=out_shape,
        compiler_params=_cparams(("arbitrary", "arbitrary") if n_side else ("parallel", "parallel")),
        name=name,
    )(*tabs, *args)
    return outs if n_side else outs[0]


def _copy_w_tile(w_hbm, wbuf, sem, step, nj, tn):
    col = pl.multiple_of((step % nj) * tn, tn)
    slot = step % WEIGHT_RING
    return pltpu.make_async_copy(w_hbm.at[:, pl.ds(col, tn)], wbuf.at[slot], sem.at[slot])


def _mm_ring_kernel(a_ref, res_ref, gate_ref, w_hbm, o_ref, wbuf, sem, *, nj, total):
    tn = o_ref.shape[1]
    s = pl.program_id(0) * nj + pl.program_id(1)

    @pl.when(s == 0)
    def _():
        for d in range(WEIGHT_RING - 1):
            _copy_w_tile(w_hbm, wbuf, sem, d, nj, tn).start()

    @pl.when(s + WEIGHT_RING - 1 < total)
    def _():
        _copy_w_tile(w_hbm, wbuf, sem, s + WEIGHT_RING - 1, nj, tn).start()

    _copy_w_tile(w_hbm, wbuf, sem, s, nj, tn).wait()
    acc = jnp.dot(a_ref[...], wbuf[s % WEIGHT_RING], preferred_element_type=F32)
    o_ref[...] = res_ref[...] + gate_ref[...] * acc


def _mm_ring_call(a, w, *, tm, tn, res, gate, rows_per_group, name):
    m = a.shape[0]
    kdim, n = w.shape
    ni, nj = m // tm, n // tn
    assert ni * nj >= WEIGHT_RING - 1 and rows_per_group % tm == 0
    tiles = rows_per_group // tm
    return pl.pallas_call(
        functools.partial(_mm_ring_kernel, nj=nj, total=ni * nj),
        grid=(ni, nj),
        in_specs=[pl.BlockSpec((tm, kdim), lambda i, j: (i, 0)),
                  pl.BlockSpec((tm, tn), lambda i, j: (i, j)),
                  pl.BlockSpec((None, 1, tn), lambda i, j: (i // tiles, 0, j)),
                  pl.BlockSpec(memory_space=pl.ANY)],
        out_specs=pl.BlockSpec((tm, tn), lambda i, j: (i, j)),
        out_shape=jax.ShapeDtypeStruct((m, n), F32),
        scratch_shapes=[pltpu.VMEM((WEIGHT_RING, kdim, tn), BF16), pltpu.SemaphoreType.DMA((WEIGHT_RING,))],
        compiler_params=pltpu.CompilerParams(dimension_semantics=("arbitrary", "arbitrary"),
                                             vmem_limit_bytes=VMEM_LIMIT_BYTES_RING),
        name=name,
    )(a, res, gate, w)


def _rope_tables(pos, head_dim):
    half = head_dim // 2
    inv = ROPE_THETA ** (-jnp.arange(half, dtype=F32) / half)
    ang = pos.astype(F32)[:, None] * inv[None, :]
    cos, sin = jnp.cos(ang), jnp.sin(ang)
    reps = LANES // head_dim
    cos_t = jnp.tile(jnp.concatenate([cos, cos], axis=1), (1, reps))
    sin_t = jnp.tile(jnp.concatenate([-sin, sin], axis=1), (1, reps))
    return cos_t, sin_t


def _rope_block(x, cos_t, sin_t, first_half):
    partner = jnp.where(first_half, pltpu.roll(x, 96, 1), pltpu.roll(x, 32, 1))
    return x * cos_t + partner * sin_t


def _first_half_mask(rows):
    lane = lax.broadcasted_iota(jnp.int32, (rows, LANES), 1)
    return (lane % 64) < 32


def _rope_k_kernel(k_ref, cos_ref, sin_ref, o_ref):
    rows, width = k_ref.shape
    fh = _first_half_mask(rows)
    cos_t, sin_t = cos_ref[...], sin_ref[...]
    for c in range(width // LANES):
        sl = slice(c * LANES, (c + 1) * LANES)
        o_ref[:, sl] = _rope_block(k_ref[:, sl].astype(F32), cos_t, sin_t, fh)


def _rope_k_call(proj, col_block, width, cos_t, sin_t, rows_per_group):
    t = proj.shape[0]
    tr = _pick(rows_per_group, (1024, 512, 256, 128, 64, 32, 16, 8))
    tiles = rows_per_group // tr
    return pl.pallas_call(
        _rope_k_kernel,
        grid=(t // tr,),
        in_specs=[pl.BlockSpec((tr, width), lambda i: (i, col_block)),
                  pl.BlockSpec((tr, LANES), lambda i: (i % tiles, 0)),
                  pl.BlockSpec((tr, LANES), lambda i: (i % tiles, 0))],
        out_specs=pl.BlockSpec((tr, width), lambda i: (i, 0)),
        out_shape=jax.ShapeDtypeStruct((t, width), F32),
        compiler_params=_cparams(("parallel",)),
        name="rope_k",
    )(proj, cos_t, sin_t)


def _gla_kernel(q_ref, k_ref, v_ref, ga_ref, ra_ref, wgk_ref, bgk_ref, g_ref, s0_ref,
                o_ref, sout_ref, s_scr, *, n_steps, c, scale, hp):
    n = pl.program_id(2)
    rows = q_ref.shape[0]
    dk = q_ref.shape[1] // hp
    dv = v_ref.shape[1] // hp
    nr = rows // c
    nt = (((1,), (1,)), ((), ()))
    tn = (((0,), (0,)), ((), ()))
    heads = range(hp)
    ksl = lambda u: slice(u * dk, (u + 1) * dk)
    vsl = lambda u: slice(u * dv, (u + 1) * dv)

    @pl.when(n == 0)
    def _():
        s_scr[...] = s0_ref[...]

    ra = ra_ref[...]
    lane = lax.broadcasted_iota(jnp.int32, ra.shape, 1)
    ra_hi = ra.astype(BF16)
    ra_lo = (ra - ra_hi.astype(F32)).astype(BF16)
    lhs = jnp.where(jnp.logical_and(lane >= GLA_GATE_RANK, lane < 2 * GLA_GATE_RANK), ra_lo, ra_hi)
    z = jnp.dot(lhs, wgk_ref[...], preferred_element_type=F32) + bgk_ref[...]
    log_a = (jnp.minimum(z, 0.0) - jnp.log(1.0 + jnp.exp(-jnp.abs(z)))) * (1.0 / GLA_GATE_NORM)

    ri = lax.broadcasted_iota(jnp.int32, (rows, 2 * rows), 0)
    ci = lax.broadcasted_iota(jnp.int32, (rows, 2 * rows), 1) % rows
    tril2 = jnp.where(jnp.logical_and(ri // c == ci // c, ci <= ri), 1.0, 0.0).astype(BF16)
    la_hi = log_a.astype(BF16)
    la_lo = (log_a - la_hi.astype(F32)).astype(BF16)
    b_all = jnp.dot(tril2, jnp.concatenate([la_hi, la_lo], axis=0), preferred_element_type=F32)

    row = lax.broadcasted_iota(jnp.int32, (c, c), 0)
    col = lax.broadcasted_iota(jnp.int32, (c, c), 1)
    causal = row >= col

    vs, qds, kds, kd32, o_parts = [], [], [], [], []
    cum = [None]
    for r in range(nr):
        sl = slice(r * c, (r + 1) * c)
        b = b_all[sl, :]
        b_mid = b[c // 2:c // 2 + 1, :]
        b_last = b[c - 1:c, :]
        q = q_ref[sl, :].astype(F32) * scale
        k = k_ref[sl, :].astype(F32)
        v = v_ref[sl, :]
        qs = (q * jnp.exp(b - b_mid)).astype(BF16)
        ks = (k * jnp.exp(b_mid - b)).astype(BF16)
        sc = [lax.dot_general(qs[:, ksl(u)], ks[:, ksl(u)], nt, preferred_element_type=F32) for u in heads]
        o = [jnp.dot(jnp.where(causal, sc[u], 0.0).astype(BF16), v[:, vsl(u)], preferred_element_type=F32)
             for u in heads]
        qd = q * jnp.exp(b)
        for rp in range(r):
            fac = cum[r] - cum[rp + 1] if rp + 1 < r else None
            qx = (qd if fac is None else qd * jnp.exp(fac)).astype(BF16)
            sx = [lax.dot_general(qx[:, ksl(u)], kds[rp][:, ksl(u)], nt, preferred_element_type=F32)
                  for u in heads]
            o = [o[u] + jnp.dot(sx[u].astype(BF16), vs[rp][:, vsl(u)], preferred_element_type=F32)
                 for u in heads]
        qds.append(qd if cum[r] is None else qd * jnp.exp(cum[r]))
        kd = k * jnp.exp(b_last - b)
        kd32.append(kd)
        kds.append(kd.astype(BF16))
        vs.append(v)
        o_parts.append(o)
        cum.append(b_last if cum[r] is None else cum[r] + b_last)

    q_all = jnp.concatenate(qds, axis=0).astype(BF16) if nr > 1 else qds[0].astype(BF16)
    total = cum[nr]
    k_parts = []
    for r in range(nr):
        if r + 1 < nr:
            k_parts.append((kd32[r] * jnp.exp(total - cum[r + 1])).astype(BF16))
        else:
            k_parts.append(kds[r])
    k_all = jnp.concatenate(k_parts, axis=0) if nr > 1 else k_parts[0]
    dec = jnp.exp(total)
    wblk = 2 * LANES
    dec_w = []
    for u in heads:
        dec_col = jnp.transpose(jnp.broadcast_to(dec[:, ksl(u)], (LANES, dk)))
        dec_w.append(jnp.concatenate([dec_col] * (wblk // LANES), axis=1))
    o_inter = [[] for _ in heads]
    for j in range(dv // wblk):
        for u in heads:
            sl = slice(j * wblk, (j + 1) * wblk)
            vcol = slice(u * dv + j * wblk, u * dv + (j + 1) * wblk)
            s_blk = s_scr[u, :, sl]
            o_inter[u].append(jnp.dot(q_all[:, ksl(u)], s_blk.astype(BF16), preferred_element_type=F32))
            upd = lax.dot_general(k_all[:, ksl(u)], v_ref[:, vcol], tn, preferred_element_type=F32)
            s_scr[u, :, sl] = s_blk * dec_w[u] + upd

    g = g_ref[...]
    for r in range(nr):
        sl = slice(r * c, (r + 1) * c)
        for u in heads:
            o = o_parts[r][u] + jnp.concatenate([p[sl, :] for p in o_inter[u]], axis=1)
            on = o * lax.rsqrt(jnp.mean(o * o, axis=-1, keepdims=True) + NORM_EPS) * g
            o_ref[sl, vsl(u)] = (on * _silu(ga_ref[sl, vsl(u)].astype(F32))).astype(o_ref.dtype)

    @pl.when(n == n_steps - 1)
    def _():
        sout_ref[...] = s_scr[...]


def _gla_call(proj, ra3, wgk3, b_gk, g_gla_out, s0, batch, seq):
    h = GLA_HEADS
    hp = GLA_HEADS_PER_STEP
    hg = h // hp
    dk = wgk3.shape[1] // h
    dv = g_gla_out.shape[0]
    c = min(CHUNK, seq)
    rows = c * GLA_BLOCK_CHUNKS if seq % (c * GLA_BLOCK_CHUNKS) == 0 else c
    ns = seq // rows
    t = batch * seq
    wk, wv = hp * dk, hp * dv
    k_off = (h * dk) // wk
    v_off = (2 * h * dk) // wv
    g_off = v_off + hg
    row = lambda b, hh, n: b * ns + n
    state_mode = dict(pipeline_mode=pl.Buffered(1)) if ns > 1 else {}
    kern = functools.partial(_gla_kernel, n_steps=ns, c=c, scale=dk ** -0.5, hp=hp)
    return pl.pallas_call(
        kern,
        grid=(batch, hg, ns),
        in_specs=[
            pl.BlockSpec((rows, wk), lambda b, hh, n: (row(b, hh, n), hh)),
            pl.BlockSpec((rows, wk), lambda b, hh, n: (row(b, hh, n), k_off + hh)),
            pl.BlockSpec((rows, wv), lambda b, hh, n: (row(b, hh, n), v_off + hh)),
            pl.BlockSpec((rows, wv), lambda b, hh, n: (row(b, hh, n), g_off + hh)),
            pl.BlockSpec((rows, LANES), lambda b, hh, n: (row(b, hh, n), 0)),
            pl.BlockSpec((LANES, wk), lambda b, hh, n: (0, hh)),
            pl.BlockSpec((1, wk), lambda b, hh, n: (0, hh)),
            pl.BlockSpec((1, dv), lambda b, hh, n: (0, 0)),
            pl.BlockSpec((None, hp, dk, dv), lambda b, hh, n: (b, hh, 0, 0), **state_mode),
        ],
        out_specs=[
            pl.BlockSpec((rows, wv), lambda b, hh, n: (row(b, hh, n), hh)),
            pl.BlockSpec((None, hp, dk, dv), lambda b, hh, n: (b, hh, 0, 0), **state_mode),
        ],
        out_shape=[jax.ShapeDtypeStruct((t, h * dv), BF16),
                   jax.ShapeDtypeStruct((batch, h, dk, dv), F32)],
        scratch_shapes=[pltpu.VMEM((hp, dk, dv), F32)],
        compiler_params=_cparams(("parallel", "parallel", "arbitrary")),
        name="gla_chunked",
    )(proj, proj, proj, proj, ra3, wgk3, b_gk.reshape(1, -1), g_gla_out.reshape(1, -1), s0)


def _dup_half(x, half, lane_lo):
    rolled = pltpu.roll(x, 64, 1)
    return jnp.where(lane_lo == (half == 0), x, rolled)


def _swa_kernel(*refs, tq, n_prev, nsub, masked):
    bias_ref, q_ref, cos_ref, sin_ref = refs[0:4]
    ka_refs = refs[4:4 + n_prev]
    va_refs = refs[4 + n_prev:4 + 2 * n_prev]
    kb_ref, vb_ref, oa_ref, za_ref, zb_ref, o_ref = refs[4 + 2 * n_prev:]

    hd = LANES // 2
    grows = SWA_GROUP * tq
    fh = _first_half_mask(tq)

    lane_q = lax.broadcasted_iota(jnp.int32, (tq, LANES), 1) < hd
    lane_a = lax.broadcasted_iota(jnp.int32, (WINDOW, LANES), 1) < hd
    ones = jnp.ones((2 * LANES, LANES), BF16)
    zpad = jnp.zeros((LANES - tq, LANES), BF16)
    lane_g = lax.broadcasted_iota(jnp.int32, (grows, LANES), 1)
    nt = (((1,), (1,)), ((), ()))

    def chunk(prev_refs, own_ref, idx, sl):
        if idx < n_prev:
            return prev_refs[idx][:, sl]
        return own_ref[(idx - n_prev) * tq:(idx - n_prev + 1) * tq, sl]

    for s in range(nsub):
        rs = slice(s * tq, (s + 1) * tq)
        cos_t, sin_t = cos_ref[rs, :], sin_ref[rs, :]
        if masked:
            valid_a = lane_g >= (WINDOW - (pl.program_id(1) * nsub + s) * tq)

        def keys_vals(h):
            pc, half = h // 2, h % 2
            sl = slice(pc * LANES, (pc + 1) * LANES)
            if n_prev == 1:
                ka, va = ka_refs[0][:, sl], va_refs[0][:, sl]
                kb, vb = kb_ref[rs, sl], vb_ref[rs, sl]
            else:
                ka = jnp.concatenate([chunk(ka_refs, kb_ref, s, sl), chunk(ka_refs, kb_ref, s + 1, sl)], axis=0)
                va = jnp.concatenate([chunk(va_refs, vb_ref, s, sl), chunk(va_refs, vb_ref, s + 1, sl)], axis=0)
                kb, vb = chunk(ka_refs, kb_ref, s + 2, sl), chunk(va_refs, vb_ref, s + 2, sl)
            k_all = jnp.concatenate([_dup_half(ka.astype(F32), half, lane_a).astype(BF16),
                                     _dup_half(kb.astype(F32), half, lane_q).astype(BF16), zpad], axis=0)
            v_all = jnp.concatenate([_dup_half(va.astype(F32), half, lane_a).astype(BF16),
                                     _dup_half(vb.astype(F32), half, lane_q).astype(BF16), zpad], axis=0)
            return k_all, v_all

        def queries(h):
            rows = []
            for g in range(SWA_GROUP):
                cb = h * (SWA_GROUP // 2) + g // 2
                qsl = slice(cb * LANES, (cb + 1) * LANES)
                qr = _rope_block(q_ref[rs, qsl].astype(F32), cos_t, sin_t, fh)
                rows.append(jnp.where(lane_q == (g % 2 == 0), qr, 0.0).astype(BF16))
            return jnp.concatenate(rows, axis=0)

        def finish(h, out2):
            for j in range(SWA_GROUP // 2):
                cb = h * (SWA_GROUP // 2) + j
                osl = slice(cb * LANES, (cb + 1) * LANES)
                ob = jnp.where(lane_q, out2[(2 * j) * tq:(2 * j + 1) * tq, :],
                               out2[(2 * j + 1) * tq:(2 * j + 2) * tq, :])
                merged = (_sigmoid(za_ref[rs, osl].astype(F32)) * oa_ref[rs, osl].astype(F32)
                          + _sigmoid(zb_ref[rs, osl].astype(F32)) * ob)
                o_ref[rs, osl] = merged.astype(o_ref.dtype)

        hs_per = SWA_HEADS_PER_STAGE
        for h0 in range(0, SWA_KV_HEADS, hs_per):
            hs = range(h0, h0 + hs_per)
            kv = [keys_vals(h) for h in hs]
            lhs = [queries(h) for h in hs]
            logits = [lax.dot_general(lhs[i], kv[i][0], nt, preferred_element_type=F32) for i in range(hs_per)]
            l1 = [jnp.where(valid_a, lg[:, :LANES], -jnp.inf) if masked else lg[:, :LANES] for lg in logits]
            l2 = [logits[i][:, LANES:] + bias_ref[h0 + i] for i in range(hs_per)]
            m = [jnp.max(jnp.maximum(l1[i], l2[i]), axis=-1, keepdims=True) for i in range(hs_per)]
            p = [jnp.concatenate([jnp.exp2(l1[i] - m[i]), jnp.exp2(l2[i] - m[i])], axis=1).astype(BF16)
                 for i in range(hs_per)]
            pvd = [jnp.dot(p[i], jnp.concatenate([kv[i][1], ones], axis=1), preferred_element_type=F32)
                   for i in range(hs_per)]
            for i in range(hs_per):
                finish(h0 + i, pvd[i][:, :LANES] / pvd[i][:, LANES:])


def _swa_call(proj, q_blk, k_rot, o_a, sinks, cos_t, sin_t, batch, seq, cache_k=None, cache_v=None):
    t, d = o_a.shape
    kvw = k_rot.shape[1]
    v_blk = ((q_blk + 3) * d) // kvw + 1
    if cache_k is None:
        tq = CHUNK
        nsub = SWA_BLOCKS_PER_STEP if seq % (SWA_BLOCKS_PER_STEP * tq) == 0 else 1
        nq = seq // (nsub * tq)
        rowi = lambda b, n: b * nq + n
        n_prev = WINDOW // CHUNK
        prev = lambda b, n, back: b * (seq // tq) + jnp.maximum(n * nsub - back, 0)
        ka_args = [k_rot] * n_prev
        va_args = [proj] * n_prev
        ka_specs = [pl.BlockSpec((tq, kvw), functools.partial(lambda b, n, back: (prev(b, n, back), 0),
                                                               back=n_prev - i)) for i in range(n_prev)]
        va_specs = [pl.BlockSpec((tq, kvw), functools.partial(lambda b, n, back: (prev(b, n, back), v_blk),
                                                               back=n_prev - i)) for i in range(n_prev)]
        masked = True
    else:
        tq = seq
        nsub = 1
        nq = 1
        rowi = lambda b, n: b
        n_prev = 1
        ka_args, va_args = [cache_k], [cache_v]
        ka_specs = [pl.BlockSpec((None, WINDOW, kvw), lambda b, n: (b, 0, 0))]
        va_specs = [pl.BlockSpec((None, WINDOW, kvw), lambda b, n: (b, 0, 0))]
        masked = False
    rows = nsub * tq
    wide = lambda blk: pl.BlockSpec((rows, d), lambda b, n: (rowi(b, n), blk))
    tab = pl.BlockSpec((rows, LANES), lambda b, n: (n, 0))
    lane = jnp.arange(LANES)
    sink_l2 = (sinks.astype(F32) * LOG2E).reshape(SWA_KV_HEADS, SWA_GROUP, 1, 1)
    bias_row = jnp.where(lane < tq, 0.0, jnp.where(lane == tq, sink_l2, -jnp.inf))
    sinks = jnp.broadcast_to(bias_row, (SWA_KV_HEADS, SWA_GROUP, tq, LANES)).reshape(
        SWA_KV_HEADS, SWA_GROUP * tq, LANES)
    q_scale = (d // SWA_HEADS) ** -0.5 * LOG2E
    cos_q, sin_q = cos_t * q_scale, sin_t * q_scale
    in_specs = ([pl.BlockSpec(sinks.shape, lambda b, n: (0, 0, 0)), wide(q_blk), tab, tab] + ka_specs + va_specs +
                [pl.BlockSpec((rows, kvw), lambda b, n: (rowi(b, n), 0)),
                 pl.BlockSpec((rows, kvw), lambda b, n: (rowi(b, n), v_blk)),
                 wide(0), wide(q_blk + 1), wide(q_blk + 2)])
    args = [sinks, proj, cos_q, sin_q] + ka_args + va_args + [k_rot, proj, o_a, proj, proj]
    return pl.pallas_call(
        functools.partial(_swa_kernel, tq=tq, n_prev=n_prev, nsub=nsub, masked=masked),
        grid=(batch, nq),
        in_specs=in_specs,
        out_specs=pl.BlockSpec((rows, d), lambda b, n: (rowi(b, n), 0)),
        out_shape=jax.ShapeDtypeStruct((t, d), BF16),
        compiler_params=_cparams(("parallel", "arbitrary")),
        name="swa_merge",
    )(*args)


def _prep_weights(w_in, w_out, w_gate_up, w_down, w_gk_up, d):
    h = GLA_HEADS
    key = (d // (2 * h)) * h
    n1 = 2 * key + 2 * d
    kvw = SWA_KV_HEADS * (d // SWA_HEADS)
    w = kvw
    nb1 = n1 // w
    per_d = d // w
    q_s = list(range(per_d))
    k_s, v_s = per_d, per_d + 1
    za_s = [per_d + 2 + i for i in range(per_d)]
    zb_s = [2 * per_d + 2 + i for i in range(per_d)]
    shifted = q_s + za_s + zb_s + [k_s, v_s]
    src_blk = list(range(nb1)) + [nb1 + s for s in shifted]
    mode = [0] * nb1 + [1] * len(shifted)
    w_in_t = jnp.transpose(w_in)
    w12 = _relayout_t_call(w_in_t, w, src_blk, mode, shift=GLA_GATE_RANK, tk=2048)

    ra_cols = jnp.transpose(w_in_t[n1:n1 + GLA_GATE_RANK, :])
    w_r = jnp.pad(jnp.concatenate([ra_cols] * 3, axis=1),
                  ((0, 0), (0, LANES - 3 * GLA_GATE_RANK))).astype(BF16)
    wg_hi = w_gk_up.astype(BF16)
    wg_lo = (w_gk_up - wg_hi.astype(F32)).astype(BF16)
    wgk3 = jnp.pad(jnp.concatenate([wg_hi, wg_hi, wg_lo], axis=0),
                   ((0, LANES - 3 * GLA_GATE_RANK), (0, 0)))

    lanes_blk = list(range(w_out.shape[1] // LANES))
    job_o = (w_out, w_out.shape[0], LANES, lanes_blk, [0] * len(lanes_blk))

    dff = w_down.shape[0]
    n_src = dff // LANES
    n_tiles = -(-dff // FF_TILE)
    per_tile = FF_TILE // LANES
    src_blk, mode = [], []
    for j in range(n_tiles):
        for part in range(2):
            for sb in range(per_tile):
                blk = j * per_tile + sb
                ok = blk < n_src
                src_blk.append(part * n_src + blk if ok else 0)
                mode.append(0 if ok else 2)
    job_gu = (w_gate_up, w_gate_up.shape[0], LANES, src_blk, mode)
    cols_dn = list(range(d // 256))
    job_dn = (w_down, dff // 4, 256, cols_dn, [0] * len(cols_dn))
    return w12, w_r, wgk3, job_o, job_gu, job_dn


def _run_job(job):
    src, tr, width, src_blk, mode = job
    return _relayout_call(src, width, src_blk, mode, tr=tr)


def _layer(x, mod, pos, s0, cache_k, cache_v, weights, params, batch, seq):
    w12, w_r, wgk3, w_o, w_gu, w_dn = weights
    g_norm1, b_gk, g_gla_out, sinks, g_norm2 = params
    t, d = x.shape
    sh1, sc1, gt1, sh2, sc2, gt2 = [m.reshape(batch, 1, d) for m in jnp.split(mod, 6, axis=-1)]
    kvw = SWA_KV_HEADS * (d // SWA_HEADS)
    hd = d // SWA_HEADS
    tm = min(MM_TILE_M, t)
    q_blk = (w12.shape[1] - 3 * d - 2 * kvw) // d

    def n_blocks(job):
        return (job[0].shape[0] // job[1]) * len(job[3])

    h = _norm_call(x, g_norm1, sc1, sh1, seq, BF16)
    steps = (t // tm) * (w12.shape[1] // 1024)
    if isinstance(w_gu, tuple) and max(n_blocks(w_gu), n_blocks(w_o)) <= steps:
        proj, w_gu, w_o = _mm_call(h, w12, tm=tm, tn=1024, out_dtype=BF16, name="in_proj", side=(w_gu, w_o))
    else:
        proj = _mm_call(h, w12, tm=tm, tn=1024, out_dtype=BF16, name="in_proj")
        if isinstance(w_gu, tuple):
            w_gu, w_o = _run_job(w_gu), _run_job(w_o)
    ra3 = _mm_call(h, w_r, tm=tm, tn=LANES, name="in_proj_rank")

    o_a, s_new = _gla_call(proj, ra3, wgk3, b_gk, g_gla_out, s0, batch, seq)

    cos_t, sin_t = _rope_tables(pos, hd)
    k_blk = ((q_blk + 3) * d) // kvw
    k_rot = _rope_k_call(proj, k_blk, kvw, cos_t, sin_t, seq)
    merged = _swa_call(proj, q_blk, k_rot, o_a, sinks, cos_t, sin_t, batch, seq, cache_k, cache_v)

    x1 = _mm_call(merged, w_o, tm=tm, tn=1024, epilogue="residual", res=x, gate=gt1,
                  rows_per_group=seq, name="out_proj")
    h2 = _norm_call(x1, g_norm2, sc2, sh2, seq, BF16)
    steps = (t // tm) * (w_gu.shape[1] // (2 * FF_TILE))
    if isinstance(w_dn, tuple) and n_blocks(w_dn) <= steps:
        act, w_dn = _mm_call(h2, w_gu, tm=tm, tn=2 * FF_TILE, epilogue="swiglu", out_dtype=BF16,
                             name="ffn_gate_up", side=(w_dn,))
    else:
        act = _mm_call(h2, w_gu, tm=tm, tn=2 * FF_TILE, epilogue="swiglu", out_dtype=BF16, name="ffn_gate_up")
        if isinstance(w_dn, tuple):
            w_dn = _run_job(w_dn)
    tm_dn = min(MM_TILE_M // 2, t)
    if seq % tm_dn == 0 and (t // tm_dn) * (d // 512) >= WEIGHT_RING:
        x2 = _mm_ring_call(act, w_dn, tm=tm_dn, tn=512, res=x1, gate=gt2, rows_per_group=seq, name="ffn_down")
    else:
        x2 = _mm_call(act, w_dn, tm=tm_dn, tn=512, epilogue="residual", res=x1, gate=gt2,
                      rows_per_group=seq, name="ffn_down")

    keep = WINDOW if cache_k is None else seq
    k_keep = jnp.stack([k_rot[(b + 1) * seq - keep:(b + 1) * seq] for b in range(batch)]).reshape(
        batch, keep, SWA_KV_HEADS, hd)
    v_cols = slice((k_blk + 1) * kvw, (k_blk + 2) * kvw)
    v_new = jnp.stack([proj[(b + 1) * seq - keep:(b + 1) * seq, v_cols] for b in range(batch)])
    v_keep = v_new.astype(F32).reshape(batch, keep, SWA_KV_HEADS, hd)
    return x2, s_new, k_keep, v_keep, (w12, w_r, wgk3, w_o, w_gu, w_dn)


def kernel(x_prompt, x_sample, state_gla, cache_swa_k, cache_swa_v, c_prompt, c_sample, w_ada, b_ada,
           g_norm1, w_in, w_gk_up, b_gk, g_gla_out, swa_sinks, w_out, g_norm2, w_gate_up, w_down, g_final):
    bp, sp, d = x_prompt.shape
    bs, ss, _ = x_sample.shape
    depth = w_ada.shape[0]
    dk = w_gk_up.shape[2] // GLA_HEADS
    dv = g_gla_out.shape[1]
    kvw = SWA_KV_HEADS * (d // SWA_HEADS)

    pos_p = jnp.arange(sp)
    pos_s = PAST_LEN + jnp.arange(ss)
    s0_p = jnp.zeros((bp, GLA_HEADS, dk, dv), F32)

    nc = bp + bs
    nc_pad = -(-nc // 16) * 16
    c_all = jnp.concatenate([c_prompt, c_sample, jnp.zeros((nc_pad - nc, d), F32)], axis=0)

    hp = x_prompt.reshape(bp * sp, d)
    hs = x_sample.reshape(bs * ss, d)
    outs = [[] for _ in range(6)]
    for l in range(depth):
        mod = _mod_call(c_all, w_ada[l], b_ada[l])
        weights = _prep_weights(w_in[l], w_out[l], w_gate_up[l], w_down[l], w_gk_up[l], d)
        params = (g_norm1[l], b_gk[l], g_gla_out[l], swa_sinks[l], g_norm2[l])
        hp, s_p, k_p, v_p, weights = _layer(hp, mod[:bp], pos_p, s0_p, None, None, weights, params, bp, sp)
        hs, s_s, k_s, v_s, _ = _layer(hs, mod[bp:nc], pos_s, state_gla[l],
                                   cache_swa_k[l].reshape(bs, WINDOW, kvw),
                                   cache_swa_v[l].reshape(bs, WINDOW, kvw), weights, params, bs, ss)
        for lst, val in zip(outs, (s_p, k_p, v_p, s_s, k_s, v_s)):
            lst.append(val)
    y_prompt = _norm_call(hp, g_final, None, None, sp, F32).reshape(bp, sp, d)
    y_sample = _norm_call(hs, g_final, None, None, ss, F32).reshape(bs, ss, d)
    return (y_prompt, y_sample) + tuple(jnp.stack(o) for o in outs)
```
